```python
import math
import jax, jax.numpy as jnp
from jax import lax
import numpy as np

D_MODEL = 1024
BATCH = 8
SEQ = 4096
DEPTH = 1

PLE_DIM = 256
EPS = 1e-6
NEG_INF = -1e30

A_HEADS = 8
A_HEAD_DIM = 64
A_WIDTH = A_HEADS * A_HEAD_DIM
DILATED_BRANCHES = ((128, 1), (512, 4), (2048, 16))

B_HEADS = 8
B_NOPE = 64
B_ROPE = 32
B_VDIM = 64
B_WIDTH = B_HEADS * B_VDIM
Q_LORA = 384
KV_LORA = 256
ROPE_THETA = 10000.0
Q_BLOCK = 128

MIX_WIDTH = A_WIDTH + B_WIDTH
IN_WIDTH = 3 * A_WIDTH + Q_LORA + KV_LORA + B_ROPE

N_GROUPS = 4
EXPERTS_PER_GROUP = 8
N_EXPERTS = N_GROUPS * EXPERTS_PER_GROUP
TOP_K_INNER = 2
EXPERT_FF = 256

kernel_name = "hymba_dilated_mla_hmoe_encoder"


def rms_norm(x, g):
    xf = x.astype(jnp.float32)
    y = xf * lax.rsqrt(jnp.mean(xf * xf, axis=-1, keepdims=True) + EPS)
    return (y * g.astype(jnp.float32)).astype(x.dtype)


def alibi_slopes(n_heads):
    return jnp.exp2(-8.0 * (jnp.arange(n_heads, dtype=jnp.float32) + 1.0) / n_heads)


def rope_tables(seq, dim):
    inv_freq = 1.0 / (ROPE_THETA ** (jnp.arange(0, dim, 2, dtype=jnp.float32) / dim))
    ang = jnp.arange(seq, dtype=jnp.float32)[:, None] * inv_freq[None, :]
    return jnp.cos(ang), jnp.sin(ang)


def apply_rope(x, cos, sin):
    half = x.shape[-1] // 2
    x1 = x[..., :half].astype(jnp.float32)
    x2 = x[..., half:].astype(jnp.float32)
    return jnp.concatenate([x1 * cos - x2 * sin, x1 * sin + x2 * cos], axis=-1).astype(x.dtype)


def banded_attention(q, k, v, dist_slopes, half):
    N, L, H, Dh = q.shape
    W = half
    nb = -(-L // W)
    Lp = nb * W
    qb = jnp.pad(q, ((0, 0), (0, Lp - L), (0, 0), (0, 0))).reshape(N, nb, W, H, Dh)
    pad_kv = ((0, 0), (W, Lp - L + W), (0, 0), (0, 0))
    kp = jnp.pad(k, pad_kv)
    vp = jnp.pad(v, pad_kv)
    kb = jnp.concatenate([kp[:, s * W: s * W + Lp].reshape(N, nb, W, H, Dh) for s in range(3)], axis=2)
    vb = jnp.concatenate([vp[:, s * W: s * W + Lp].reshape(N, nb, W, H, Dh) for s in range(3)], axis=2)
    scores = jnp.einsum('nbqhd,nbkhd->nbhqk', qb, kb).astype(jnp.float32)
    qi = jnp.arange(W)[:, None]
    kc = jnp.arange(3 * W)[None, :]
    rel = kc - W - qi
    j = jnp.arange(nb)[:, None] * W + jnp.arange(3 * W)[None, :] - W
    mask = (jnp.abs(rel) <= half)[None, :, :] & ((j >= 0) & (j < L))[:, None, :]
    bias = -dist_slopes[:, None, None] * jnp.abs(rel).astype(jnp.float32)[None]
    scores = jnp.where(mask[None, :, None], scores + bias[None, None], NEG_INF)
    m = jnp.max(scores, axis=-1, keepdims=True)
    e = jnp.exp(scores - m)
    s = jnp.sum(e, axis=-1, keepdims=True)
    out = jnp.einsum('nbhqk,nbkhd->nbqhd', (e / s).astype(v.dtype), vb)
    lse = (m + jnp.log(s))[..., 0]
    out = out.reshape(N, Lp, H, Dh)[:, :L]
    lse = lse.transpose(0, 1, 3, 2).reshape(N, Lp, H)[:, :L]
    return out, lse


def dilated_mixture_attention(q, k, v):
    B, S, H, Dh = q.shape
    q = q * (Dh ** -0.5)
    slopes = alibi_slopes(H)
    outs, lses = [], []
    for window, dil in DILATED_BRANCHES:
        half = window // (2 * dil)
        L = S // dil

        def to_sub(t):
            return t.reshape(B, L, dil, H, Dh).transpose(0, 2, 1, 3, 4).reshape(B * dil, L, H, Dh)

        o, lse = banded_attention(to_sub(q), to_sub(k), to_sub(v), slopes * dil, half)
        outs.append(o.reshape(B, dil, L, H, Dh).transpose(0, 2, 1, 3, 4).reshape(B, S, H, Dh))
        lses.append(lse.reshape(B, dil, L, H).transpose(0, 2, 1, 3).reshape(B, S, H))
    w = jax.nn.softmax(jnp.stack(lses), axis=0)
    return jnp.einsum('nbsh,nbshd->bshd', w.astype(q.dtype), jnp.stack(outs))


def mla_attention(qn, qr, kn, kr, v):
    B, S, H, _ = qn.shape
    scale = (B_NOPE + B_ROPE) ** -0.5
    nq = S // Q_BLOCK
    qn_b = qn.reshape(B, nq, Q_BLOCK, H, B_NOPE).transpose(1, 0, 2, 3, 4)
    qr_b = qr.reshape(B, nq, Q_BLOCK, H, B_ROPE).transpose(1, 0, 2, 3, 4)

    def attend(args):
        qn_i, qr_i = args
        s = jnp.einsum('bqhd,bkhd->bhqk', qn_i, kn) + jnp.einsum('bqhr,bkr->bhqk', qr_i, kr)
        pr = jax.nn.softmax(s.astype(jnp.float32) * scale, axis=-1).astype(v.dtype)
        return jnp.einsum('bhqk,bkhd->bqhd', pr, v)

    out = lax.map(attend, (qn_b, qr_b))
    return out.transpose(1, 0, 2, 3, 4).reshape(B, S, H * B_VDIM)


def hierarchical_moe(t, w_r1, b_r1, w_r2, b_r2, w_e_gate, w_e_up, w_e_down):
    l1 = (t @ w_r1 + b_r1).astype(jnp.float32)
    p1 = jax.nn.softmax(l1, axis=-1)
    g_val, g_idx = lax.top_k(p1, 1)
    l2_all = jnp.einsum('td,gde->tge', t, w_r2) + b_r2
    l2 = jnp.take_along_axis(l2_all, g_idx[:, :, None], axis=1)[:, 0].astype(jnp.float32)
    p2 = jax.nn.softmax(l2, axis=-1)
    e_val, e_idx = lax.top_k(p2, TOP_K_INNER)
    gate = g_val * e_val / jnp.sum(e_val, axis=-1, keepdims=True)
    eid = g_idx * EXPERTS_PER_GROUP + e_idx
    cw = jnp.sum(jax.nn.one_hot(eid, N_EXPERTS, dtype=jnp.float32) * gate[:, :, None], axis=1)
    cw = cw.astype(t.dtype)
    y = jnp.zeros_like(t)
    for e in range(N_EXPERTS):
        h = jax.nn.silu(t @ w_e_gate[e]) * (t @ w_e_up[e])
        y = y + cw[:, e:e + 1] * (h @ w_e_down[e])
    return y


def setup_inputs(seed: int = 0) -> dict:
    key = jax.random.key(seed)
    ks = jax.random.split(key, 24)
    f32 = jnp.float32

    def nrm(k, shape, fan_in):
        return jax.random.normal(k, shape, f32) * (fan_in ** -0.5)

    def gain(k, shape):
        return 1.0 + 0.02 * jax.random.normal(k, shape, f32)

    return {
        "x": jax.random.normal(ks[0], (BATCH, SEQ, D_MODEL), f32),
        "p": jax.random.normal(ks[1], (DEPTH, BATCH, SEQ, PLE_DIM), f32),
        "g_mix": gain(ks[2], (DEPTH, D_MODEL)),
        "w_in": nrm(ks[3], (DEPTH, D_MODEL, IN_WIDTH), D_MODEL),
        "g_cq": gain(ks[4], (DEPTH, Q_LORA)),
        "w_uq": nrm(ks[5], (DEPTH, Q_LORA, B_HEADS * (B_NOPE + B_ROPE)), Q_LORA),
        "g_ckv": gain(ks[6], (DEPTH, KV_LORA)),
        "w_ukv": nrm(ks[7], (DEPTH, KV_LORA, B_HEADS * (B_NOPE + B_VDIM)), KV_LORA),
        "g_out_a": gain(ks[8], (DEPTH, A_WIDTH)),
        "g_out_b": gain(ks[9], (DEPTH, B_WIDTH)),
        "w_o": nrm(ks[10], (DEPTH, MIX_WIDTH, D_MODEL), MIX_WIDTH),
        "g_ffn": gain(ks[11], (DEPTH, D_MODEL)),
        "w_r1": nrm(ks[12], (DEPTH, D_MODEL, N_GROUPS), D_MODEL),
        "b_r1": 0.01 * jax.random.normal(ks[13], (DEPTH, N_GROUPS), f32),
        "w_r2": nrm(ks[14], (DEPTH, N_GROUPS, D_MODEL, EXPERTS_PER_GROUP), D_MODEL),
        "b_r2": 0.01 * jax.random.normal(ks[15], (DEPTH, N_GROUPS, EXPERTS_PER_GROUP), f32),
        "w_e_gate": nrm(ks[16], (DEPTH, N_EXPERTS, D_MODEL, EXPERT_FF), D_MODEL),
        "w_e_up": nrm(ks[17], (DEPTH, N_EXPERTS, D_MODEL, EXPERT_FF), D_MODEL),
        "w_e_down": nrm(ks[18], (DEPTH, N_EXPERTS, EXPERT_FF, D_MODEL), EXPERT_FF),
        "g_ple": gain(ks[19], (DEPTH, D_MODEL)),
        "w_ple_gate": nrm(ks[20], (DEPTH, D_MODEL, D_MODEL), D_MODEL),
        "w_ple_proj": nrm(ks[21], (DEPTH, PLE_DIM, D_MODEL), PLE_DIM),
        "g_final": gain(ks[22], (D_MODEL,)),
    }


def reference(x, p, g_mix, w_in, g_cq, w_uq, g_ckv, w_ukv, g_out_a, g_out_b, w_o,
              g_ffn, w_r1, b_r1, w_r2, b_r2, w_e_gate, w_e_up, w_e_down,
              g_ple, w_ple_gate, w_ple_proj, g_final):
    B, S, D = x.shape
    cos, sin = rope_tables(S, B_ROPE)
    h = x
    for i in range(DEPTH):
        a = rms_norm(h, g_mix[i])
        proj = a @ w_in[i]
        o = 0
        qa = proj[..., o:o + A_WIDTH].reshape(B, S, A_HEADS, A_HEAD_DIM); o += A_WIDTH
        ka = proj[..., o:o + A_WIDTH].reshape(B, S, A_HEADS, A_HEAD_DIM); o += A_WIDTH
        va = proj[..., o:o + A_WIDTH].reshape(B, S, A_HEADS, A_HEAD_DIM); o += A_WIDTH
        c_q = proj[..., o:o + Q_LORA]; o += Q_LORA
        c_kv = proj[..., o:o + KV_LORA]; o += KV_LORA
        k_rope = proj[..., o:o + B_ROPE]

        out_a = dilated_mixture_attention(qa, ka, va).reshape(B, S, A_WIDTH)

        q_b = (rms_norm(c_q, g_cq[i]) @ w_uq[i]).reshape(B, S, B_HEADS, B_NOPE + B_ROPE)
        q_nope = q_b[..., :B_NOPE]
        q_rope = apply_rope(q_b[..., B_NOPE:], cos[:, None, :], sin[:, None, :])
        kv_b = (rms_norm(c_kv, g_ckv[i]) @ w_ukv[i]).reshape(B, S, B_HEADS, B_NOPE + B_VDIM)
        k_nope = kv_b[..., :B_NOPE]
        v_b = kv_b[..., B_NOPE:]
        k_rope = apply_rope(k_rope, cos, sin)
        out_b = mla_attention(q_nope, q_rope, k_nope, k_rope, v_b)

        mixed = jnp.concatenate([rms_norm(out_a, g_out_a[i]), rms_norm(out_b, g_out_b[i])], axis=-1)
        h = h + mixed @ w_o[i]

        m = rms_norm(h, g_ffn[i]).reshape(B * S, D)
        h = h + hierarchical_moe(m, w_r1[i], b_r1[i], w_r2[i], b_r2[i],
                                 w_e_gate[i], w_e_up[i], w_e_down[i]).reshape(B, S, D)

        gate = jax.nn.sigmoid(rms_norm(h, g_ple[i]) @ w_ple_gate[i])
        h = h + gate * (p[i] @ w_ple_proj[i])
    return rms_norm(h, g_final)
```

```python
import functools
import math

import jax
import jax.numpy as jnp
from jax import lax
from jax.experimental import pallas as pl
from jax.experimental.pallas import tpu as pltpu

F32 = jnp.float32
BF16 = jnp.bfloat16

EPS = 1e-6
MASK_VALUE = -1e30
LANES = 128

A_HEADS = 8
A_HEAD_DIM = 64
A_WIDTH = A_HEADS * A_HEAD_DIM
DILATED_BRANCHES = ((128, 1), (512, 4), (2048, 16))

B_HEADS = 8
B_NOPE = 64
B_ROPE = 32
B_VDIM = 64
B_WIDTH = B_HEADS * B_VDIM
ROPE_THETA = 10000.0

N_GROUPS = 4
EXPERTS_PER_GROUP = 8
N_EXPERTS = N_GROUPS * EXPERTS_PER_GROUP

VMEM_LIMIT = 56 * 1024 * 1024

TOKEN_TILE = 256
BAND_TILE = 128
BAND_HALF = 64
BAND_KEYS = 256
MLA_TQ = 256
MLA_TK = 512
EXPERT_ROWS = 256


def _rms(x, g):
    return x * lax.rsqrt(jnp.mean(x * x, axis=-1, keepdims=True) + EPS) * g


def _params(*sem):
    return pltpu.CompilerParams(dimension_semantics=sem, vmem_limit_bytes=VMEM_LIMIT)


def _in_proj_kernel(x_ref, gmix_ref, w1_ref, gcq_ref, wqm_ref, wqs_ref, gckv_ref, wk_ref,
                    e_ref, wv_ref, cosq_ref, sinq_ref, cosk_ref, sink_ref,
                    qa_ref, ka_ref, va_ref, qf_ref, kf_ref, vb_ref, *, dq, dkv):
    a = _rms(x_ref[...], gmix_ref[...]).astype(BF16)
    proj = jnp.dot(a, w1_ref[...], preferred_element_type=F32)
    aw = A_WIDTH
    qa_ref[...] = (proj[:, 0:aw] * (A_HEAD_DIM ** -0.5)).astype(BF16)
    ka_ref[...] = proj[:, aw:2 * aw].astype(BF16)
    va_ref[...] = proj[:, 2 * aw:3 * aw].astype(BF16)
    o = 3 * aw
    cq = _rms(proj[:, o:o + dq], gcq_ref[...]).astype(BF16)
    o += dq
    ckv = _rms(proj[:, o:o + dkv], gckv_ref[...]).astype(BF16)
    o += dkv
    kr = proj[:, o:o + LANES]
    kr_sw = proj[:, o + LANES:o + 2 * LANES]

    qm = jnp.dot(cq, wqm_ref[...], preferred_element_type=F32)
    qs = jnp.dot(cq, wqs_ref[...], preferred_element_type=F32)
    cosq = cosq_ref[...]
    sinq = sinq_ref[...]
    for h in range(B_HEADS):
        sl = slice(h * LANES, (h + 1) * LANES)
        qf_ref[:, sl] = (qm[:, sl] * cosq + qs[:, sl] * sinq).astype(BF16)

    krot = (kr * cosk_ref[...] + kr_sw * sink_ref[...]).astype(BF16)
    kf = jnp.dot(ckv, wk_ref[...], preferred_element_type=F32)
    kf = kf + jnp.dot(krot, e_ref[...], preferred_element_type=F32)
    kf_ref[...] = kf.astype(BF16)
    vb_ref[...] = jnp.dot(ckv, wv_ref[...], preferred_element_type=F32).astype(BF16)


def _in_proj(x2, gmix, w1, gcq, wqm, wqs, gckv, wk, e, wv, cosq, sinq, cosk, sink, seq):
    t, d = x2.shape
    tm = TOKEN_TILE
    n_pos = seq // tm
    dq, dkv = wqm.shape[0], wk.shape[0]
    full = lambda arr: pl.BlockSpec(arr.shape, lambda i: (0, 0))
    tab = pl.BlockSpec((tm, LANES), lambda i: (i % n_pos, 0))
    row = lambda w: pl.BlockSpec((tm, w), lambda i: (i, 0))
    outs = [(A_WIDTH, BF16)] * 3 + [(B_HEADS * LANES, BF16)] * 2 + [(B_WIDTH, BF16)]
    return pl.pallas_call(
        functools.partial(_in_proj_kernel, dq=dq, dkv=dkv),
        grid=(t // tm,),
        in_specs=[row(d), full(gmix), full(w1), full(gcq), full(wqm), full(wqs), full(gckv),
                  full(wk), full(e), full(wv), tab, tab, tab, tab],
        out_specs=[row(w) for w, _ in outs],
        out_shape=[jax.ShapeDtypeStruct((t, w), dt) for w, dt in outs],
        compiler_params=_params("parallel"),
        name="in_proj",
    )(x2, gmix, w1, gcq, wqm, wqs, gckv, wk, e, wv, cosq, sinq, cosk, sink)


def _band_bias(slopes_scaled):
    qi = jnp.arange(BAND_TILE)[:, None]
    kc = jnp.arange(BAND_KEYS)[None, :]
    variants = []
    for shift in (0, BAND_HALF, BAND_KEYS - BAND_TILE):
        rel = jnp.abs(kc - shift - qi).astype(F32)
        bias = -slopes_scaled[:, None, None] * rel[None]
        variants.append(jnp.where(rel[None] <= BAND_HALF, bias, MASK_VALUE))
    return jnp.stack(variants, axis=1)


def _dilated_kernel(q_ref, k_ref, v_ref, bias_ref, o_ref, lse_ref, *, cls_len, chunk):
    c = pl.program_id(2)
    lane = lax.broadcasted_iota(jnp.int32, (BAND_TILE, LANES), 1)
    low_half = lane < A_HEAD_DIM
    lse_lane_head = lane // (LANES // A_HEADS)

    def tile(t, carry):
        row0 = pl.multiple_of(t * BAND_TILE, BAND_TILE)
        l0 = c * chunk + row0
        start = jnp.clip(l0 - BAND_HALF, 0, cls_len - BAND_KEYS)
        start = pl.multiple_of(start, BAND_HALF)
        variant = jnp.where(l0 == 0, 0, jnp.where(l0 == cls_len - BAND_TILE, 2, 1))
        q = q_ref[0, pl.ds(row0, BAND_TILE), :]
        kw = k_ref[0, pl.ds(start, BAND_KEYS), :]
        vw = v_ref[0, pl.ds(start, BAND_KEYS), :]
        lse_tile = jnp.zeros((BAND_TILE, LANES), F32)
        for hp in range(A_HEADS // 2):
            sl = slice(hp * LANES, (hp + 1) * LANES)
            qp, kp, vp = q[:, sl], kw[:, sl], vw[:, sl]
            outs = []
            for hh in range(2):
                h = 2 * hp + hh
                qh = jnp.where(low_half if hh == 0 else ~low_half, qp, jnp.zeros_like(qp))
                s = lax.dot_general(qh, kp, (((1,), (1,)), ((), ())), preferred_element_type=F32)
                s = s + bias_ref[h, variant]
                m = jnp.max(s, axis=-1, keepdims=True)
                p = jnp.exp(s - m)
                den = jnp.sum(p, axis=-1, keepdims=True)
                pv = jnp.dot(p.astype(BF16), vp, preferred_element_type=F32)
                outs.append(pv / den)
                lse_tile = jnp.where(lse_lane_head == h, m + jnp.log(den), lse_tile)
            o_ref[0, pl.ds(row0, BAND_TILE), sl] = jnp.where(low_half, outs[0], outs[1]).astype(BF16)
        lse_ref[0, pl.ds(row0, BAND_TILE), :] = lse_tile
        return carry

    lax.fori_loop(0, chunk // BAND_TILE, tile, 0)


def _dilated_branch(qa, ka, va, bias, batch, seq, dil):
    cls_len = seq // dil
    assert cls_len >= BAND_KEYS and cls_len % BAND_TILE == 0
    chunk = min(cls_len, 1024)
    view = lambda a: a.reshape(batch, cls_len, dil * A_WIDTH)
    qspec = pl.BlockSpec((1, chunk, A_WIDTH), lambda b, r, c: (b, c, r))
    kspec = pl.BlockSpec((1, cls_len, A_WIDTH), lambda b, r, c: (b, 0, r))
    o, lse = pl.pallas_call(
        functools.partial(_dilated_kernel, cls_len=cls_len, chunk=chunk),
        grid=(batch, dil, cls_len // chunk),
        in_specs=[qspec, kspec, kspec, pl.BlockSpec(bias.shape, lambda b, r, c: (0, 0, 0, 0))],
        out_specs=[qspec, pl.BlockSpec((1, chunk, LANES), lambda b, r, c: (b, c, r))],
        out_shape=[jax.ShapeDtypeStruct((batch, cls_len, dil * A_WIDTH), BF16),
                   jax.ShapeDtypeStruct((batch, cls_len, dil * LANES), F32)],
        compiler_params=_params("parallel", "parallel", "parallel"),
        name=f"dilated_d{dil}",
    )(view(qa), view(ka), view(va), bias)
    return o.reshape(batch * seq, A_WIDTH), lse.reshape(batch * seq, LANES)


def _mla_kernel(q_ref, k_ref, v_ref, o_ref, *, seq):
    lane = lax.broadcasted_iota(jnp.int32, (MLA_TQ, LANES), 1)
    outs = []
    for hh in range(2):
        sl = slice(hh * LANES, (hh + 1) * LANES)
        qh = q_ref[0, :, sl]

        def chunk(c, carry):
            m, l, acc = carry
            k0 = pl.multiple_of(c * MLA_TK, MLA_TK)
            kc = k_ref[0, pl.ds(k0, MLA_TK), sl]
            vc = v_ref[0, pl.ds(k0, MLA_TK), :]
            s = lax.dot_general(qh, kc, (((1,), (1,)), ((), ())), preferred_element_type=F32)
            m_new = jnp.maximum(m, jnp.max(s, axis=-1, keepdims=True))
            alpha = jnp.exp(m - m_new)
            p = jnp.exp(s - m_new)
            l = alpha * l + jnp.sum(p, axis=-1, keepdims=True)
            acc = alpha * acc + jnp.dot(p.astype(BF16), vc, preferred_element_type=F32)
            return m_new, l, acc

        init = (jnp.full((MLA_TQ, 1), MASK_VALUE, F32), jnp.zeros((MLA_TQ, 1), F32),
                jnp.zeros((MLA_TQ, LANES), F32))
        _, l, acc = lax.fori_loop(0, seq // MLA_TK, chunk, init)
        outs.append(acc / l)
    o_ref[0] = jnp.where(lane < B_VDIM, outs[0], outs[1]).astype(BF16)


def _mla(qf, kf, vb, batch, seq):
    pairs = B_HEADS // 2
    return pl.pallas_call(
        functools.partial(_mla_kernel, seq=seq),
        grid=(batch, pairs, seq // MLA_TQ),
        in_specs=[pl.BlockSpec((1, MLA_TQ, 2 * LANES), lambda b, hp, i: (b, i, hp)),
                  pl.BlockSpec((1, seq, 2 * LANES), lambda b, hp, i: (b, 0, hp)),
                  pl.BlockSpec((1, seq, LANES), lambda b, hp, i: (b, 0, hp))],
        out_specs=pl.BlockSpec((1, MLA_TQ, LANES), lambda b, hp, i: (b, i, hp)),
        out_shape=jax.ShapeDtypeStruct((batch, seq, B_WIDTH), BF16),
        compiler_params=_params("parallel", "parallel", "parallel"),
        name="mla",
    )(qf.reshape(batch, seq, -1), kf.reshape(batch, seq, -1), vb.reshape(batch, seq, -1))


def _post_mix_kernel(x_ref, o1_ref, o2_ref, o3_ref, l1_ref, l2_ref, l3_ref, ob_ref, expand_ref,
                     ga_ref, gb_ref, woa_ref, wob_ref, gffn_ref, wr_ref, br_ref, tri_ref,
                     h_ref, m_ref, route_ref, cnt_ref, carry_ref):
    i = pl.program_id(0)

    @pl.when(i == 0)
    def _():
        carry_ref[...] = jnp.zeros_like(carry_ref)

    lses = (l1_ref[...], l2_ref[...], l3_ref[...])
    top = jnp.maximum(jnp.maximum(lses[0], lses[1]), lses[2])
    es = [jnp.exp(l - top) for l in lses]
    inv = 1.0 / (es[0] + es[1] + es[2])
    out_a = jnp.zeros(o1_ref.shape, F32)
    for e, o_ref in zip(es, (o1_ref, o2_ref, o3_ref)):
        w = jnp.dot(e * inv, expand_ref[...], preferred_element_type=F32,
                    precision=lax.Precision.HIGHEST)
        out_a = out_a + w * o_ref[...].astype(F32)

    na = _rms(out_a, ga_ref[...]).astype(BF16)
    nb = _rms(ob_ref[...].astype(F32), gb_ref[...]).astype(BF16)
    h = x_ref[...] + jnp.dot(na, woa_ref[...], preferred_element_type=F32)
    h = h + jnp.dot(nb, wob_ref[...], preferred_element_type=F32)
    h_ref[...] = h
    m = _rms(h, gffn_ref[...])
    m_ref[...] = m

    logits = jnp.dot(m, wr_ref[...], preferred_element_type=F32,
                     precision=lax.Precision.HIGHEST) + br_ref[...]
    tm = logits.shape[0]
    lane_i = lax.broadcasted_iota(jnp.int32, (tm, LANES), 1)
    lane = lane_i.astype(F32)
    big = float(LANES)
    is_group = (lane_i >= N_EXPERTS) & (lane_i < N_EXPERTS + N_GROUPS)
    lg = jnp.where(is_group, logits, -jnp.inf)
    g_max = jnp.max(lg, axis=-1, keepdims=True)
    g_idx = jnp.min(jnp.where(lg == g_max, lane - N_EXPERTS, big), axis=-1, keepdims=True)
    g_val = 1.0 / jnp.sum(jnp.exp(lg - g_max), axis=-1, keepdims=True)

    group_of_lane = (lane_i // EXPERTS_PER_GROUP).astype(F32)
    in_group = (lane_i < N_EXPERTS) & (group_of_lane == g_idx)
    le = jnp.where(in_group, logits, -jnp.inf)
    e_max = jnp.max(le, axis=-1, keepdims=True)
    ee = jnp.exp(le - e_max)
    p2 = jnp.where(in_group, ee / jnp.sum(ee, axis=-1, keepdims=True), -1.0)
    v1 = jnp.max(p2, axis=-1, keepdims=True)
    i1 = jnp.min(jnp.where(p2 == v1, lane, big), axis=-1, keepdims=True)
    p2b = jnp.where(lane == i1, -1.0, p2)
    v2 = jnp.max(p2b, axis=-1, keepdims=True)
    i2 = jnp.min(jnp.where(p2b == v2, lane, big), axis=-1, keepdims=True)
    norm = g_val / (v1 + v2)
    gate1 = v1 * norm
    gate2 = v2 * norm

    oh1 = (lane == i1)
    oh2 = (lane == i2)
    oh1b = oh1.astype(BF16)
    oh2b = oh2.astype(BF16)
    pre1 = jnp.dot(tri_ref[...], oh1b, preferred_element_type=F32)
    pre2 = jnp.dot(tri_ref[...], oh2b, preferred_element_type=F32)
    cnt1 = jnp.sum(oh1.astype(F32), axis=0, keepdims=True)
    cnt2 = jnp.sum(oh2.astype(F32), axis=0, keepdims=True)
    carry = carry_ref[...]
    rank1 = jnp.sum(jnp.where(oh1, carry + pre1, 0.0), axis=-1, keepdims=True)
    rank2 = jnp.sum(jnp.where(oh2, carry + cnt1 + pre2, 0.0), axis=-1, keepdims=True)
    carry = carry + cnt1 + cnt2
    carry_ref[...] = carry
    cnt_ref[...] = jnp.broadcast_to(carry, cnt_ref.shape)

    route = jnp.where(lane_i == 0, i1, 0.0)
    route = jnp.where(lane_i == 1, i2, route)
    route = jnp.where(lane_i == 2, gate1, route)
    route = jnp.where(lane_i == 3, gate2, route)
    route = jnp.where(lane_i == 4, rank1, route)
    route = jnp.where(lane_i == 5, rank2, route)
    route_ref[...] = route


def _post_mix(x2, o1, o2, o3, l1, l2, l3, ob, expand, ga, gb, woa, wob, gffn, wr, br, tri):
    t, d = x2.shape
    tm = TOKEN_TILE
    full = lambda arr: pl.BlockSpec(arr.shape, lambda i: (0, 0))
    row = lambda w: pl.BlockSpec((tm, w), lambda i: (i, 0))
    return pl.pallas_call(
        _post_mix_kernel,
        grid=(t // tm,),
        in_specs=[row(d), row(A_WIDTH), row(A_WIDTH), row(A_WIDTH), row(LANES), row(LANES),
                  row(LANES), row(B_WIDTH), full(expand), full(ga), full(gb), full(woa), full(wob),
                  full(gffn), full(wr), full(br), full(tri)],
        out_specs=[row(d), row(d), row(LANES), pl.BlockSpec((8, LANES), lambda i: (0, 0))],
        out_shape=[jax.ShapeDtypeStruct((t, d), F32), jax.ShapeDtypeStruct((t, d), F32),
                   jax.ShapeDtypeStruct((t, LANES), F32), jax.ShapeDtypeStruct((8, LANES), F32)],
        scratch_shapes=[pltpu.VMEM((1, LANES), F32)],
        compiler_params=_params("arbitrary"),
        name="post_mix",
    )(x2, o1, o2, o3, l1, l2, l3, ob, expand, ga, gb, woa, wob, gffn, wr, br, tri)


def _row_copy(src, src_row, dst, dst_row, sem):
    return pltpu.make_async_copy(src.at[pl.ds(src_row, 1), :], dst.at[pl.ds(dst_row, 1), :], sem)


def _dispatch_kernel(pos_ref, m_ref, xs_ref, sem):
    tm = m_ref.shape[0]

    def issue(j, carry):
        for k in range(2):
            _row_copy(m_ref, j, xs_ref, pos_ref[2 * j + k], sem).start()
        return carry

    def drain(j, carry):
        for k in range(2):
            _row_copy(m_ref, j, xs_ref, pos_ref[2 * j + k], sem).wait()
        return carry

    lax.fori_loop(0, tm, issue, 0)
    lax.fori_loop(0, tm, drain, 0)


def _dispatch(pos_flat, m):
    t, d = m.shape
    tm = TOKEN_TILE
    return pl.pallas_call(
        _dispatch_kernel,
        grid=(t // tm,),
        in_specs=[pl.BlockSpec((2 * tm,), lambda i: (i,), memory_space=pltpu.SMEM),
                  pl.BlockSpec((tm, d), lambda i: (i, 0))],
        out_specs=pl.BlockSpec(memory_space=pl.ANY),
        out_shape=jax.ShapeDtypeStruct((2 * t, d), F32),
        scratch_shapes=[pltpu.SemaphoreType.DMA(())],
        compiler_params=_params("arbitrary"),
        name="dispatch",
    )(pos_flat, m)


def _experts_kernel(tile_ref, exp_ref, valid_ref, first_ref, offs_ref,
                    xs_ref, wg_ref, wu_ref, wd_ref, ys_ref):
    w = pl.program_id(0)

    @pl.when(valid_ref[w] == 1)
    def _():
        x = xs_ref[...].astype(BF16)
        g = jnp.dot(x, wg_ref[0], preferred_element_type=F32)
        u = jnp.dot(x, wu_ref[0], preferred_element_type=F32)
        hid = (g * (1.0 / (1.0 + jnp.exp(-g))) * u).astype(BF16)
        y = jnp.dot(hid, wd_ref[0], preferred_element_type=F32)
        e = exp_ref[w]
        rows = tile_ref[w] * EXPERT_ROWS + lax.broadcasted_iota(jnp.int32, (EXPERT_ROWS, 1), 0)
        mine = (rows >= offs_ref[e]) & (rows < offs_ref[e + 1])

        @pl.when(first_ref[w] == 1)
        def _():
            ys_ref[...] = jnp.where(mine, y, 0.0)

        @pl.when(first_ref[w] == 0)
        def _():
            ys_ref[...] = jnp.where(mine, y, ys_ref[...])


def _experts(tile_ids, exp_ids, valid, first, offs, xs, wg, wu, wd):
    p, d = xs.shape
    ff = wg.shape[-1]
    n_items = tile_ids.shape[0]
    grid_spec = pltpu.PrefetchScalarGridSpec(
        num_scalar_prefetch=5,
        grid=(n_items,),
        in_specs=[pl.BlockSpec((EXPERT_ROWS, d), lambda w, ti, ex, va, fi, of: (ti[w], 0)),
                  pl.BlockSpec((1, d, ff), lambda w, ti, ex, va, fi, of: (ex[w], 0, 0)),
                  pl.BlockSpec((1, d, ff), lambda w, ti, ex, va, fi, of: (ex[w], 0, 0)),
                  pl.BlockSpec((1, ff, d), lambda w, ti, ex, va, fi, of: (ex[w], 0, 0))],
        out_specs=pl.BlockSpec((EXPERT_ROWS, d), lambda w, ti, ex, va, fi, of: (ti[w], 0)),
    )
    return pl.pallas_call(
        _experts_kernel,
        grid_spec=grid_spec,
        out_shape=jax.ShapeDtypeStruct((p, d), F32),
        compiler_params=_params("arbitrary"),
        name="experts",
    )(tile_ids, exp_ids, valid, first, offs, xs, wg, wu, wd)


def _work_items(counts, n_rows):
    n_tiles = n_rows // EXPERT_ROWS
    n_items = n_tiles + N_EXPERTS - 1
    offs = jnp.concatenate([jnp.zeros((1,), jnp.int32), jnp.cumsum(counts)])
    first_tile = offs[:-1] // EXPERT_ROWS
    last_tile = (offs[1:] - 1) // EXPERT_ROWS
    per_expert = jnp.where(counts > 0, last_tile - first_tile + 1, 0)
    item_end = jnp.cumsum(per_expert)
    item_start = item_end - per_expert
    w = jnp.arange(n_items, dtype=jnp.int32)
    valid = w < item_end[-1]
    e = jnp.sum((item_end[None, :] <= w[:, None]).astype(jnp.int32), axis=1)
    e = jnp.minimum(e, N_EXPERTS - 1)
    tile = first_tile[e] + (w - item_start[e])
    last_valid = jnp.maximum(item_end[-1] - 1, 0)
    tile = jnp.where(valid, tile, tile[last_valid]).astype(jnp.int32)
    e = jnp.where(valid, e, e[last_valid])
    first = jnp.concatenate([jnp.ones((1,), jnp.int32), (tile[1:] != tile[:-1]).astype(jnp.int32)])
    return tile, e, valid.astype(jnp.int32), first, offs.astype(jnp.int32)


def _finish_kernel(pos_ref, ys_ref, route_ref, h_ref, p_ref, gple_ref, wpg_ref, wpp_ref, gfin_ref,
                   out_ref, rows_ref, sem):
    tm = h_ref.shape[0]

    def issue(j, carry):
        for k in range(2):
            _row_copy(ys_ref, pos_ref[2 * j + k], rows_ref.at[k], j, sem).start()
        return carry

    def drain(j, carry):
        for k in range(2):
            _row_copy(ys_ref, pos_ref[2 * j + k], rows_ref.at[k], j, sem).wait()
        return carry

    lax.fori_loop(0, tm, issue, 0)
    lax.fori_loop(0, tm, drain, 0)

    route = route_ref[...]
    h = h_ref[...] + route[:, 2:3] * rows_ref[0] + route[:, 3:4] * rows_ref[1]
    gate_in = _rms(h, gple_ref[...]).astype(BF16)
    z = jnp.dot(gate_in, wpg_ref[...], preferred_element_type=F32)
    gate = 1.0 / (1.0 + jnp.exp(-z))
    h = h + gate * jnp.dot(p_ref[...].astype(BF16), wpp_ref[...], preferred_element_type=F32)
    out_ref[...] = _rms(h, gfin_ref[...])


def _finish(pos_flat, ys, route, h, p2, gple, wpg, wpp, gfin):
    t, d = h.shape
    tm = TOKEN_TILE
    full = lambda arr: pl.BlockSpec(arr.shape, lambda i: (0, 0))
    row = lambda w: pl.BlockSpec((tm, w), lambda i: (i, 0))
    return pl.pallas_call(
        _finish_kernel,
        grid=(t // tm,),
        in_specs=[pl.BlockSpec((2 * tm,), lambda i: (i,), memory_space=pltpu.SMEM),
                  pl.BlockSpec(memory_space=pl.ANY), row(LANES), row(d), row(p2.shape[1]),
                  full(gple), full(wpg), full(wpp), full(gfin)],
        out_specs=row(d),
        out_shape=jax.ShapeDtypeStruct((t, d), F32),
        scratch_shapes=[pltpu.VMEM((2, tm, d), F32), pltpu.SemaphoreType.DMA(())],
        compiler_params=_params("arbitrary"),
        name="finish",
    )(pos_flat, ys, route, h, p2, gple, wpg, wpp, gfin)


def _swap_halves(w):
    half = w.shape[-1] // 2
    return jnp.concatenate([w[..., half:], w[..., :half]], axis=-1)


def _layer_weights(w_in, w_uq, w_ukv, seq):
    d = w_in.shape[0]
    dq, dkv = w_uq.shape[0], w_ukv.shape[0]
    o = 3 * A_WIDTH + dq + dkv
    w_kr = w_in[:, o:o + B_ROPE]
    pad = jnp.zeros((d, LANES - B_ROPE), F32)
    w1 = jnp.concatenate([w_in[:, :o], w_kr, pad, _swap_halves(w_kr), pad], axis=1).astype(BF16)

    uq = w_uq.reshape(dq, B_HEADS, B_NOPE + B_ROPE)
    zq = jnp.zeros((dq, B_HEADS, LANES - B_NOPE - B_ROPE), F32)
    wqm = jnp.concatenate([uq, zq], axis=-1).reshape(dq, B_HEADS * LANES).astype(BF16)
    wqs = jnp.concatenate([jnp.zeros((dq, B_HEADS, B_NOPE), F32), _swap_halves(uq[..., B_NOPE:]), zq],
                          axis=-1).reshape(dq, B_HEADS * LANES).astype(BF16)

    ukv = w_ukv.reshape(dkv, B_HEADS, B_NOPE + B_VDIM)
    wk = jnp.concatenate([ukv[..., :B_NOPE], jnp.zeros((dkv, B_HEADS, LANES - B_NOPE), F32)],
                         axis=-1).reshape(dkv, B_HEADS * LANES).astype(BF16)
    wv = ukv[..., B_NOPE:].reshape(dkv, B_WIDTH).astype(BF16)

    j = jnp.arange(LANES)[:, None]
    col = jnp.arange(B_HEADS * LANES)[None, :]
    place = ((col % LANES) == j + B_NOPE) & (j < B_ROPE)
    e = place.astype(BF16)

    inv_freq = 1.0 / (ROPE_THETA ** (jnp.arange(0, B_ROPE, 2, dtype=F32) / B_ROPE))
    ang = jnp.arange(seq, dtype=F32)[:, None] * inv_freq[None, :]
    cos, sin = jnp.cos(ang), jnp.sin(ang)
    scale = (B_NOPE + B_ROPE) ** -0.5
    zr = jnp.zeros((seq, LANES - B_NOPE - B_ROPE), F32)
    cosq = scale * jnp.concatenate([jnp.ones((seq, B_NOPE), F32), cos, cos, zr], axis=1)
    sinq = scale * jnp.concatenate([jnp.zeros((seq, B_NOPE), F32), -sin, sin, zr], axis=1)
    zk = jnp.zeros((seq, LANES - B_ROPE), F32)
    cosk = jnp.concatenate([cos, cos, zk], axis=1)
    sink = jnp.concatenate([-sin, sin, zk], axis=1)
    return w1, wqm, wqs, wk, wv, e, cosq, sinq, cosk, sink


def _router_weights(w_r1, b_r1, w_r2, b_r2):
    d = w_r1.shape[0]
    w2 = jnp.transpose(w_r2, (1, 0, 2)).reshape(d, N_EXPERTS)
    padw = jnp.zeros((d, LANES - N_EXPERTS - N_GROUPS), F32)
    wr = jnp.concatenate([w2, w_r1, padw], axis=1)
    br = jnp.concatenate([b_r2.reshape(N_EXPERTS), b_r1, jnp.zeros((LANES - N_EXPERTS - N_GROUPS,), F32)])
    return wr, br[None, :]


def _layer(h2, p2, batch, seq, g_mix, w_in, g_cq, w_uq, g_ckv, w_ukv, g_out_a, g_out_b, w_o, g_ffn,
           w_r1, b_r1, w_r2, b_r2, w_e_gate, w_e_up, w_e_down, g_ple, w_ple_gate, w_ple_proj, g_out):
    t, d = h2.shape
    w1, wqm, wqs, wk, wv, e, cosq, sinq, cosk, sink = _layer_weights(w_in, w_uq, w_ukv, seq)
    qa, ka, va, qf, kf, vb = _in_proj(h2, g_mix[None], w1, g_cq[None], wqm, wqs, g_ckv[None], wk, e, wv,
                                      cosq, sinq, cosk, sink, seq)

    slopes = jnp.exp2(-8.0 * (jnp.arange(A_HEADS, dtype=F32) + 1.0) / A_HEADS)
    branch = []
    for window, dil in DILATED_BRANCHES:
        assert window // (2 * dil) == BAND_HALF
        branch.append(_dilated_branch(qa, ka, va, _band_bias(slopes * dil), batch, seq, dil))
    ob = _mla(qf, kf, vb, batch, seq).reshape(t, B_WIDTH)

    lanes_per_head = LANES // A_HEADS
    src = jnp.arange(LANES)[:, None]
    dst = jnp.arange(A_WIDTH)[None, :]
    expand = (src == (dst // A_HEAD_DIM) * lanes_per_head).astype(F32)
    wr, br = _router_weights(w_r1, b_r1, w_r2, b_r2)
    tri = (jnp.arange(TOKEN_TILE)[:, None] > jnp.arange(TOKEN_TILE)[None, :]).astype(BF16)
    h1, m, route, cnt = _post_mix(
        h2, branch[0][0], branch[1][0], branch[2][0], branch[0][1], branch[1][1], branch[2][1], ob,
        expand, g_out_a[None], g_out_b[None], w_o[:A_WIDTH].astype(BF16), w_o[A_WIDTH:].astype(BF16),
        g_ffn[None], wr, br, tri)

    counts = cnt[0, :N_EXPERTS].astype(jnp.int32)
    tile_ids, exp_ids, valid, first, offs = _work_items(counts, 2 * t)
    eid = route[:, 0:2].astype(jnp.int32)
    pos = (offs[eid] + route[:, 4:6].astype(jnp.int32)).reshape(2 * t)
    xs = _dispatch(pos, m)
    ys = _experts(tile_ids, exp_ids, valid, first, offs, xs,
                  w_e_gate.astype(BF16), w_e_up.astype(BF16), w_e_down.astype(BF16))
    return _finish(pos, ys, route, h1, p2, g_ple[None], w_ple_gate.astype(BF16),
                   w_ple_proj.astype(BF16), g_out[None])


def kernel(x, p, g_mix, w_in, g_cq, w_uq, g_ckv, w_ukv, g_out_a, g_out_b, w_o, g_ffn, w_r1, b_r1, w_r2,
           b_r2, w_e_gate, w_e_up, w_e_down, g_ple, w_ple_gate, w_ple_proj, g_final):
    batch, seq, d = x.shape
    depth = p.shape[0]
    assert depth == 1, "the final norm is fused into the single layer's last kernel"
    h = x.reshape(batch * seq, d)
    i = 0
    h = _layer(h, p[i].reshape(batch * seq, -1), batch, seq, g_mix[i], w_in[i], g_cq[i], w_uq[i],
               g_ckv[i], w_ukv[i], g_out_a[i], g_out_b[i], w_o[i], g_ffn[i], w_r1[i], b_r1[i], w_r2[i],
               b_r2[i], w_e_gate[i], w_e_up[i], w_e_down[i], g_ple[i], w_ple_gate[i], w_ple_proj[i],
               g_final)
    return h.reshape(batch, seq, d)
```

```python
import functools
import math

import jax
import jax.numpy as jnp
from jax import lax
from jax.experimental import pallas as pl
from jax.experimental.pallas import tpu as pltpu

F32 = jnp.float32
BF16 = jnp.bfloat16

EPS = 1e-6
MASK_VALUE = -1e30
LANES = 128
SUBLANES = 8

A_HEADS = 8
A_HEAD_DIM = 64
A_WIDTH = A_HEADS * A_HEAD_DIM
DILATED_BRANCHES = ((128, 1), (512, 4), (2048, 16))

B_HEADS = 8
B_NOPE = 64
B_ROPE = 32
B_VDIM = 64
B_WIDTH = B_HEADS * B_VDIM
ROPE_THETA = 10000.0

N_GROUPS = 4
EXPERTS_PER_GROUP = 8
N_EXPERTS = N_GROUPS * EXPERTS_PER_GROUP
N_PAIR_KEYS = N_EXPERTS * EXPERTS_PER_GROUP
N_PAIRS = N_GROUPS * EXPERTS_PER_GROUP * (EXPERTS_PER_GROUP - 1) // 2

VMEM_LIMIT = 56 * 1024 * 1024

TOKEN_TILE = 256
BAND_TILE = 128
BAND_HALF = 64
BAND_KEYS = 256
BAND_UNROLL = 4
MERGE_ROWS = 256
MLA_TQ = 256
EXPERT_ROWS = 256
REC_ROWS = 2 * SUBLANES
GATE_ROW = SUBLANES


def _rms(x, g):
    return x * lax.rsqrt(jnp.mean(x * x, axis=-1, keepdims=True) + EPS) * g


def _params(*sem):
    return pltpu.CompilerParams(dimension_semantics=sem, vmem_limit_bytes=VMEM_LIMIT)


def _token_cols(ref, n_tokens, rows_per_token):
    cols = [ref[pl.ds(c, n_tokens, stride=rows_per_token), :] for c in range(SUBLANES)]
    return jnp.concatenate(cols, axis=1)


def _in_proj_kernel(x_ref, gmix_ref, w1_ref, gcq_ref, wqm_ref, wqs_ref, gckv_ref, wk_ref,
                    e_ref, wv_ref, cosq_ref, sinq_ref, cosk_ref, sink_ref,
                    qkv_ref, qf_ref, kf_ref, vb_ref, *, dq, dkv):
    a = _rms(x_ref[...], gmix_ref[...]).astype(BF16)
    proj = jnp.dot(a, w1_ref[...], preferred_element_type=F32)
    o = 3 * A_WIDTH
    qkv_ref[...] = proj[:, :o]
    cq = _rms(proj[:, o:o + dq], gcq_ref[...]).astype(BF16)
    o += dq
    ckv = _rms(proj[:, o:o + dkv], gckv_ref[...]).astype(BF16)
    o += dkv
    kr = proj[:, o:o + LANES]
    kr_sw = proj[:, o + LANES:o + 2 * LANES]

    qm = jnp.dot(cq, wqm_ref[...], preferred_element_type=F32)
    qs = jnp.dot(cq, wqs_ref[...], preferred_element_type=F32)
    cosq = cosq_ref[...]
    sinq = sinq_ref[...]
    for h in range(B_HEADS):
        sl = slice(h * LANES, (h + 1) * LANES)
        qf_ref[:, sl] = (qm[:, sl] * cosq + qs[:, sl] * sinq).astype(BF16)

    krot = (kr * cosk_ref[...] + kr_sw * sink_ref[...]).astype(BF16)
    kf = jnp.dot(ckv, wk_ref[...], preferred_element_type=F32)
    kf = kf + jnp.dot(krot, e_ref[...], preferred_element_type=F32)
    kf_ref[...] = kf.astype(BF16)
    vb_ref[...] = jnp.dot(ckv, wv_ref[...], preferred_element_type=F32).astype(BF16)


def _in_proj(x2, gmix, w1, gcq, wqm, wqs, gckv, wk, e, wv, cosq, sinq, cosk, sink, seq):
    t, d = x2.shape
    tm = TOKEN_TILE
    n_pos = seq // tm
    dq, dkv = wqm.shape[0], wk.shape[0]
    full = lambda arr: pl.BlockSpec(arr.shape, lambda i: (0, 0))
    tab = pl.BlockSpec((tm, LANES), lambda i: (i % n_pos, 0))
    row = lambda w: pl.BlockSpec((tm, w), lambda i: (i, 0))
    outs = [(3 * A_WIDTH, F32), (B_HEADS * LANES, BF16), (B_HEADS * LANES, BF16), (B_WIDTH, BF16)]
    return pl.pallas_call(
        functools.partial(_in_proj_kernel, dq=dq, dkv=dkv),
        grid=(t // tm,),
        in_specs=[row(d), full(gmix), full(w1), full(gcq), full(wqm), full(wqs), full(gckv),
                  full(wk), full(e), full(wv), tab, tab, tab, tab],
        out_specs=[row(w) for w, _ in outs],
        out_shape=[jax.ShapeDtypeStruct((t, w), dt) for w, dt in outs],
        compiler_params=_params("parallel"),
        name="in_proj",
    )(x2, gmix, w1, gcq, wqm, wqs, gckv, wk, e, wv, cosq, sinq, cosk, sink)


def _band_bias(slopes_scaled):
    qi = jnp.arange(BAND_TILE)[:, None]
    kc = jnp.arange(BAND_KEYS)[None, :]
    variants = []
    for shift in (0, BAND_HALF, BAND_KEYS - BAND_TILE):
        rel = jnp.abs(kc - shift - qi).astype(F32)
        bias = -slopes_scaled[:, None, None] * rel[None]
        variants.append(jnp.where(rel[None] <= BAND_HALF, bias, MASK_VALUE))
    return jnp.stack(variants, axis=1)


def _dilated_kernel(q_ref, k_ref, v_ref, bias_ref, out_ref, o_scr, lse_scr, *, seq):
    lane = lax.broadcasted_iota(jnp.int32, (BAND_TILE, LANES), 1)
    low_half = lane < A_HEAD_DIM

    for n, (_, dil) in enumerate(DILATED_BRANCHES):
        cls_len = seq // dil
        tiles_per_class = cls_len // BAND_TILE

        def tile(it, carry, n=n, dil=dil, cls_len=cls_len, tiles_per_class=tiles_per_class):
            r = it // tiles_per_class
            l0 = (it % tiles_per_class) * BAND_TILE
            start = jnp.clip(l0 - BAND_HALF, 0, cls_len - BAND_KEYS)
            variant = jnp.where(l0 == 0, 0, jnp.where(l0 == cls_len - BAND_TILE, 2, 1))
            if dil == 1:
                q_rows = pl.ds(pl.multiple_of(l0, BAND_TILE), BAND_TILE)
                k_rows = pl.ds(pl.multiple_of(start, BAND_HALF), BAND_KEYS)
            else:
                q_rows = pl.ds(r + dil * l0, BAND_TILE, stride=dil)
                k_rows = pl.ds(r + dil * start, BAND_KEYS, stride=dil)
            qp = q_ref[q_rows, :].astype(BF16)
            kp = k_ref[k_rows, :].astype(BF16)
            vp = v_ref[k_rows, :].astype(BF16)
            outs, lses = [], []
            for hh in range(2):
                qh = jnp.where(low_half if hh == 0 else ~low_half, qp, jnp.zeros_like(qp))
                s = lax.dot_general(qh, kp, (((1,), (1,)), ((), ())), preferred_element_type=F32)
                s = s + bias_ref[n, hh, variant]
                m = jnp.max(s, axis=-1, keepdims=True)
                p = jnp.exp(s - m)
                den = jnp.sum(p, axis=-1, keepdims=True)
                pv = jnp.dot(p.astype(BF16), vp, preferred_element_type=F32)
                outs.append(pv / den)
                lses.append(m + jnp.log(den))
            o_scr[n, q_rows, :] = jnp.where(low_half, outs[0], outs[1])
            lse_scr[n, q_rows, :] = jnp.where(low_half, lses[0], lses[1])
            return carry

        lax.fori_loop(0, seq // BAND_TILE, tile, 0, unroll=BAND_UNROLL)

    def merge(c, carry):
        rows = pl.ds(pl.multiple_of(c * MERGE_ROWS, MERGE_ROWS), MERGE_ROWS)
        lses = [lse_scr[n, rows, :] for n in range(len(DILATED_BRANCHES))]
        top = functools.reduce(jnp.maximum, lses)
        es = [jnp.exp(l - top) for l in lses]
        acc = sum(e * o_scr[n, rows, :] for n, e in enumerate(es))
        out_ref[rows, :] = (acc / sum(es)).astype(BF16)
        return carry

    lax.fori_loop(0, seq // MERGE_ROWS, merge, 0)


def _dilated(qkv, bias, batch, seq):
    for _, dil in DILATED_BRANCHES:
        assert seq // dil >= BAND_KEYS and (seq // dil) % BAND_TILE == 0
    pairs = A_HEADS // 2
    n_br = len(DILATED_BRANCHES)
    col = lambda base: pl.BlockSpec((None, seq, LANES), lambda b, hp: (b, 0, base + hp))
    return pl.pallas_call(
        functools.partial(_dilated_kernel, seq=seq),
        grid=(batch, pairs),
        in_specs=[col(0), col(pairs), col(2 * pairs),
                  pl.BlockSpec((n_br, 2, 3, BAND_TILE, BAND_KEYS), lambda b, hp: (0, hp, 0, 0, 0))],
        out_specs=col(0),
        out_shape=jax.ShapeDtypeStruct((batch, seq, A_WIDTH), BF16),
        scratch_shapes=[pltpu.VMEM((n_br, seq, LANES), F32), pltpu.VMEM((n_br, seq, LANES), F32)],
        compiler_params=_params("parallel", "parallel"),
        name="dilated",
    )(qkv, qkv, qkv, bias)


def _mla_kernel(q_ref, k_ref, v_ref, o_ref):
    lane = lax.broadcasted_iota(jnp.int32, (MLA_TQ, LANES), 1)
    outs = []
    for hh in range(2):
        sl = slice(hh * LANES, (hh + 1) * LANES)
        s = lax.dot_general(q_ref[:, sl], k_ref[:, sl], (((1,), (1,)), ((), ())),
                            preferred_element_type=F32)
        p = jnp.exp2(s - jnp.max(s, axis=-1, keepdims=True))
        den = jnp.sum(p, axis=-1, keepdims=True)
        pv = jnp.dot(p.astype(BF16), v_ref[...], preferred_element_type=F32)
        outs.append(pv / den)
    o_ref[...] = jnp.where(lane < B_VDIM, outs[0], outs[1]).astype(BF16)


def _mla(qf, kf, vb, batch, seq):
    pairs = B_HEADS // 2
    return pl.pallas_call(
        _mla_kernel,
        grid=(batch, pairs, seq // MLA_TQ),
        in_specs=[pl.BlockSpec((None, MLA_TQ, 2 * LANES), lambda b, hp, i: (b, i, hp)),
                  pl.BlockSpec((None, seq, 2 * LANES), lambda b, hp, i: (b, 0, hp)),
                  pl.BlockSpec((None, seq, LANES), lambda b, hp, i: (b, 0, hp))],
        out_specs=pl.BlockSpec((None, MLA_TQ, LANES), lambda b, hp, i: (b, i, hp)),
        out_shape=jax.ShapeDtypeStruct((batch, seq, B_WIDTH), BF16),
        compiler_params=_params("parallel", "parallel", "parallel"),
        name="mla",
    )(qf.reshape(batch, seq, -1), kf.reshape(batch, seq, -1), vb.reshape(batch, seq, -1))


def _post_mix_kernel(x_ref, oa_ref, ob_ref, ga_ref, gb_ref, woa_ref, wob_ref, gffn_ref,
                     wr2_ref, wrh_ref, br_ref, tri_ref,
                     h_ref, rec_ref, route_ref, cnt_ref, carry_ref):
    i = pl.program_id(0)

    @pl.when(i == 0)
    def _():
        carry_ref[...] = jnp.zeros_like(carry_ref)

    na = _rms(oa_ref[...].astype(F32), ga_ref[...]).astype(BF16)
    nb = _rms(ob_ref[...].astype(F32), gb_ref[...]).astype(BF16)
    h = x_ref[...] + jnp.dot(na, woa_ref[...], preferred_element_type=F32)
    h = h + jnp.dot(nb, wob_ref[...], preferred_element_type=F32)
    h_ref[...] = h
    m = _rms(h, gffn_ref[...])
    tm = m.shape[0]

    m_hi = m.astype(BF16)
    m_lo = (m - m_hi.astype(F32)).astype(BF16)
    part = jnp.dot(m_hi, wr2_ref[...], preferred_element_type=F32)
    logits = part[:, :LANES] + part[:, LANES:] + br_ref[...]
    logits = logits + jnp.dot(m_lo, wrh_ref[...], preferred_element_type=F32)

    lane_i = lax.broadcasted_iota(jnp.int32, (tm, LANES), 1)
    lane = lane_i.astype(F32)
    big = float(LANES)
    is_group = (lane_i >= N_EXPERTS) & (lane_i < N_EXPERTS + N_GROUPS)
    lg = jnp.where(is_group, logits, -jnp.inf)
    g_max = jnp.max(lg, axis=-1, keepdims=True)
    g_idx = jnp.min(jnp.where(lg == g_max, lane - N_EXPERTS, big), axis=-1, keepdims=True)
    g_val = 1.0 / jnp.sum(jnp.exp(lg - g_max), axis=-1, keepdims=True)

    group_of_lane = (lane_i // EXPERTS_PER_GROUP).astype(F32)
    in_group = (lane_i < N_EXPERTS) & (group_of_lane == g_idx)
    le = jnp.where(in_group, logits, -jnp.inf)
    e_max = jnp.max(le, axis=-1, keepdims=True)
    ee = jnp.exp(le - e_max)
    p2 = jnp.where(in_group, ee / jnp.sum(ee, axis=-1, keepdims=True), -1.0)
    v1 = jnp.max(p2, axis=-1, keepdims=True)
    i1 = jnp.min(jnp.where(p2 == v1, lane, big), axis=-1, keepdims=True)
    p2b = jnp.where(lane == i1, -1.0, p2)
    v2 = jnp.max(p2b, axis=-1, keepdims=True)
    i2 = jnp.min(jnp.where(p2b == v2, lane, big), axis=-1, keepdims=True)
    norm = g_val / (v1 + v2)
    gate1 = v1 * norm
    gate2 = v2 * norm

    first_is_lo = i1 < i2
    e_lo = jnp.where(first_is_lo, i1, i2)
    e_hi = jnp.where(first_is_lo, i2, i1)
    g_lo = jnp.where(first_is_lo, gate1, gate2)
    g_hi = jnp.where(first_is_lo, gate2, gate1)
    key = e_lo * EXPERTS_PER_GROUP + (e_hi - g_idx * EXPERTS_PER_GROUP)

    key_lane = lax.broadcasted_iota(jnp.int32, (tm, N_PAIR_KEYS), 1).astype(F32)
    onehot = key_lane == key
    prefix = jnp.dot(tri_ref[...], onehot.astype(BF16), preferred_element_type=F32)
    carry = carry_ref[...]
    rank = jnp.sum(jnp.where(onehot, carry + prefix, 0.0), axis=-1, keepdims=True)
    carry = carry + jnp.sum(onehot.astype(F32), axis=0, keepdims=True)
    carry_ref[...] = carry
    cnt_ref[...] = jnp.broadcast_to(carry, cnt_ref.shape)

    route_ref[...] = jnp.where(lane_i == 0, key, jnp.where(lane_i == 1, rank, 0.0))

    rec_ref[...] = jnp.zeros_like(rec_ref)
    for c in range(SUBLANES):
        rec_ref[pl.ds(c, tm, stride=REC_ROWS), :] = m[:, c * LANES:(c + 1) * LANES]
    rec_ref[pl.ds(GATE_ROW, tm, stride=REC_ROWS), :] = jnp.where(
        lane_i == 0, g_lo, jnp.where(lane_i == 1, g_hi, 0.0))


def _post_mix(x2, oa, ob, ga, gb, woa, wob, gffn, wr2, wrh, br, tri):
    t, d = x2.shape
    tm = TOKEN_TILE
    full = lambda arr: pl.BlockSpec(arr.shape, lambda i: (0, 0))
    row = lambda w: pl.BlockSpec((tm, w), lambda i: (i, 0))
    return pl.pallas_call(
        _post_mix_kernel,
        grid=(t // tm,),
        in_specs=[row(d), row(A_WIDTH), row(B_WIDTH), full(ga), full(gb), full(woa), full(wob),
                  full(gffn), full(wr2), full(wrh), full(br), full(tri)],
        out_specs=[row(d), pl.BlockSpec((tm * REC_ROWS, LANES), lambda i: (i, 0)), row(LANES),
                   pl.BlockSpec((SUBLANES, N_PAIR_KEYS), lambda i: (0, 0))],
        out_shape=[jax.ShapeDtypeStruct((t, d), F32),
                   jax.ShapeDtypeStruct((t * REC_ROWS, LANES), F32),
                   jax.ShapeDtypeStruct((t, LANES), F32),
                   jax.ShapeDtypeStruct((SUBLANES, N_PAIR_KEYS), F32)],
        scratch_shapes=[pltpu.VMEM((1, N_PAIR_KEYS), F32)],
        compiler_params=_params("arbitrary"),
        name="post_mix",
    )(x2, oa, ob, ga, gb, woa, wob, gffn, wr2, wrh, br, tri)


def _rows_copy(src, src_token, dst, dst_token, rows_per_token, sem):
    src_rows = pl.ds(pl.multiple_of(src_token * rows_per_token, rows_per_token), rows_per_token)
    dst_rows = pl.ds(pl.multiple_of(dst_token * rows_per_token, rows_per_token), rows_per_token)
    return pltpu.make_async_copy(src.at[src_rows, :], dst.at[dst_rows, :], sem)


def _dispatch_kernel(pos_ref, rec_ref, xs_ref, sem):
    n_tokens = pos_ref.shape[0]

    def issue(j, carry):
        _rows_copy(rec_ref, j, xs_ref, pos_ref[j], REC_ROWS, sem).start()
        return carry

    def drain(j, carry):
        _rows_copy(rec_ref, j, xs_ref, pos_ref[j], REC_ROWS, sem).wait()
        return carry

    lax.fori_loop(0, n_tokens, issue, 0)
    lax.fori_loop(0, n_tokens, drain, 0)


def _dispatch(pos, rec):
    tm = TOKEN_TILE
    t = pos.shape[0]
    return pl.pallas_call(
        _dispatch_kernel,
        grid=(t // tm,),
        in_specs=[pl.BlockSpec((tm,), lambda i: (i,), memory_space=pltpu.SMEM),
                  pl.BlockSpec((tm * REC_ROWS, LANES), lambda i: (i, 0))],
        out_specs=pl.BlockSpec(memory_space=pl.ANY),
        out_shape=jax.ShapeDtypeStruct(rec.shape, F32),
        scratch_shapes=[pltpu.SemaphoreType.DMA(())],
        compiler_params=_params("arbitrary"),
        name="dispatch",
    )(pos, rec)


def _ffn(x, wg_ref, wu_ref, wd_ref):
    g = jnp.dot(x, wg_ref[0], preferred_element_type=F32)
    u = jnp.dot(x, wu_ref[0], preferred_element_type=F32)
    hid = (g * (1.0 / (1.0 + jnp.exp(-g))) * u).astype(BF16)
    return jnp.dot(hid, wd_ref[0], preferred_element_type=F32)


def _experts_kernel(tile_ref, key_ref, valid_ref, first_ref, offs_ref,
                    xs_ref, wg_lo, wu_lo, wd_lo, wg_hi, wu_hi, wd_hi, ys_ref):
    w = pl.program_id(0)
    rows_n = EXPERT_ROWS

    @pl.when(valid_ref[w] == 1)
    def _():
        x = _token_cols(xs_ref, rows_n, REC_ROWS).astype(BF16)
        gates = xs_ref[pl.ds(GATE_ROW, rows_n, stride=REC_ROWS), :]
        y = gates[:, 0:1] * _ffn(x, wg_lo, wu_lo, wd_lo) + gates[:, 1:2] * _ffn(x, wg_hi, wu_hi, wd_hi)
        key = key_ref[w]
        rows = tile_ref[w] * rows_n + lax.broadcasted_iota(jnp.int32, (rows_n, 1), 0)
        mine = (rows >= offs_ref[key]) & (rows < offs_ref[key + 1])

        @pl.when(first_ref[w] == 1)
        def _():
            for c in range(SUBLANES):
                ys_ref[pl.ds(c, rows_n, stride=SUBLANES), :] = jnp.where(
                    mine, y[:, c * LANES:(c + 1) * LANES], 0.0)

        @pl.when(first_ref[w] == 0)
        def _():
            for c in range(SUBLANES):
                rows_c = pl.ds(c, rows_n, stride=SUBLANES)
                ys_ref[rows_c, :] = jnp.where(mine, y[:, c * LANES:(c + 1) * LANES], ys_ref[rows_c, :])


def _experts(tile_ids, keys, valid, first, offs, xs, wg, wu, wd):
    n_tok = xs.shape[0] // REC_ROWS
    d, ff = wg.shape[1], wg.shape[2]
    n_items = tile_ids.shape[0]
    lo = lambda w, ti, ke, va, fi, of: (ke[w] // EXPERTS_PER_GROUP, 0, 0)
    hi = lambda w, ti, ke, va, fi, of: (
        ke[w] // (EXPERTS_PER_GROUP * EXPERTS_PER_GROUP) * EXPERTS_PER_GROUP + ke[w] % EXPERTS_PER_GROUP, 0, 0)
    grid_spec = pltpu.PrefetchScalarGridSpec(
        num_scalar_prefetch=5,
        grid=(n_items,),
        in_specs=[pl.BlockSpec((EXPERT_ROWS * REC_ROWS, LANES), lambda w, ti, ke, va, fi, of: (ti[w], 0)),
                  pl.BlockSpec((1, d, ff), lo), pl.BlockSpec((1, d, ff), lo), pl.BlockSpec((1, ff, d), lo),
                  pl.BlockSpec((1, d, ff), hi), pl.BlockSpec((1, d, ff), hi), pl.BlockSpec((1, ff, d), hi)],
        out_specs=pl.BlockSpec((EXPERT_ROWS * SUBLANES, LANES), lambda w, ti, ke, va, fi, of: (ti[w], 0)),
    )
    return pl.pallas_call(
        _experts_kernel,
        grid_spec=grid_spec,
        out_shape=jax.ShapeDtypeStruct((n_tok * SUBLANES, LANES), F32),
        compiler_params=_params("arbitrary"),
        name="experts",
    )(tile_ids, keys, valid, first, offs, xs, wg, wu, wd, wg, wu, wd)


def _work_items(counts, n_rows):
    n_keys = counts.shape[0]
    n_tiles = n_rows // EXPERT_ROWS
    n_items = n_tiles + N_PAIRS - 1
    offs = jnp.concatenate([jnp.zeros((1,), jnp.int32), jnp.cumsum(counts)])
    first_tile = offs[:-1] // EXPERT_ROWS
    last_tile = (offs[1:] - 1) // EXPERT_ROWS
    per_key = jnp.where(counts > 0, last_tile - first_tile + 1, 0)
    item_end = jnp.cumsum(per_key)
    item_start = item_end - per_key
    w = jnp.arange(n_items, dtype=jnp.int32)
    valid = w < item_end[-1]
    key = jnp.sum((item_end[None, :] <= w[:, None]).astype(jnp.int32), axis=1)
    key = jnp.minimum(key, n_keys - 1)
    tile = first_tile[key] + (w - item_start[key])
    last_valid = jnp.maximum(item_end[-1] - 1, 0)
    tile = jnp.where(valid, tile, tile[last_valid]).astype(jnp.int32)
    key = jnp.where(valid, key, key[last_valid])
    first = jnp.concatenate([jnp.ones((1,), jnp.int32), (tile[1:] != tile[:-1]).astype(jnp.int32)])
    return tile, key, valid.astype(jnp.int32), first, offs.astype(jnp.int32)


def _finish_kernel(pos_ref, ys_ref, h_ref, p_ref, gple_ref, wpg_ref, wpp_ref, gfin_ref,
                   out_ref, rows_ref, sem):
    tm = h_ref.shape[0]

    def issue(j, carry):
        _rows_copy(ys_ref, pos_ref[j], rows_ref, j, SUBLANES, sem).start()
        return carry

    def drain(j, carry):
        _rows_copy(ys_ref, pos_ref[j], rows_ref, j, SUBLANES, sem).wait()
        return carry

    lax.fori_loop(0, tm, issue, 0)
    lax.fori_loop(0, tm, drain, 0)

    h = h_ref[...] + _token_cols(rows_ref, tm, SUBLANES)
    gate_in = _rms(h, gple_ref[...]).astype(BF16)
    z = jnp.dot(gate_in, wpg_ref[...], preferred_element_type=F32)
    gate = 1.0 / (1.0 + jnp.exp(-z))
    h = h + gate * jnp.dot(p_ref[...].astype(BF16), wpp_ref[...], preferred_element_type=F32)
    out_ref[...] = _rms(h, gfin_ref[...])


def _finish(pos, ys, h, p2, gple, wpg, wpp, gfin):
    t, d = h.shape
    tm = TOKEN_TILE
    full = lambda arr: pl.BlockSpec(arr.shape, lambda i: (0, 0))
    row = lambda w: pl.BlockSpec((tm, w), lambda i: (i, 0))
    return pl.pallas_call(
        _finish_kernel,
        grid=(t // tm,),
        in_specs=[pl.BlockSpec((tm,), lambda i: (i,), memory_space=pltpu.SMEM),
                  pl.BlockSpec(memory_space=pl.ANY), row(d), row(p2.shape[1]),
                  full(gple), full(wpg), full(wpp), full(gfin)],
        out_specs=row(d),
        out_shape=jax.ShapeDtypeStruct((t, d), F32),
        scratch_shapes=[pltpu.VMEM((tm * SUBLANES, LANES), F32), pltpu.SemaphoreType.DMA(())],
        compiler_params=_params("arbitrary"),
        name="finish",
    )(pos, ys, h, p2, gple, wpg, wpp, gfin)


def _swap_halves(w):
    half = w.shape[-1] // 2
    return jnp.concatenate([w[..., half:], w[..., :half]], axis=-1)


def _layer_weights(w_in, w_uq, w_ukv, seq):
    d = w_in.shape[0]
    dq, dkv = w_uq.shape[0], w_ukv.shape[0]
    o = 3 * A_WIDTH + dq + dkv
    w_kr = w_in[:, o:o + B_ROPE]
    pad = jnp.zeros((d, LANES - B_ROPE), F32)
    w_q = w_in[:, :A_WIDTH] * (A_HEAD_DIM ** -0.5)
    w1 = jnp.concatenate([w_q, w_in[:, A_WIDTH:o], w_kr, pad, _swap_halves(w_kr), pad], axis=1).astype(BF16)

    uq = w_uq.reshape(dq, B_HEADS, B_NOPE + B_ROPE)
    zq = jnp.zeros((dq, B_HEADS, LANES - B_NOPE - B_ROPE), F32)
    wqm = jnp.concatenate([uq, zq], axis=-1).reshape(dq, B_HEADS * LANES).astype(BF16)
    wqs = jnp.concatenate([jnp.zeros((dq, B_HEADS, B_NOPE), F32), _swap_halves(uq[..., B_NOPE:]), zq],
                          axis=-1).reshape(dq, B_HEADS * LANES).astype(BF16)

    ukv = w_ukv.reshape(dkv, B_HEADS, B_NOPE + B_VDIM)
    wk = jnp.concatenate([ukv[..., :B_NOPE], jnp.zeros((dkv, B_HEADS, LANES - B_NOPE), F32)],
                         axis=-1).reshape(dkv, B_HEADS * LANES).astype(BF16)
    wv = ukv[..., B_NOPE:].reshape(dkv, B_WIDTH).astype(BF16)

    j = jnp.arange(LANES)[:, None]
    col = jnp.arange(B_HEADS * LANES)[None, :]
    place = ((col % LANES) == j + B_NOPE) & (j < B_ROPE)
    e = place.astype(BF16)

    inv_freq = 1.0 / (ROPE_THETA ** (jnp.arange(0, B_ROPE, 2, dtype=F32) / B_ROPE))
    ang = jnp.arange(seq, dtype=F32)[:, None] * inv_freq[None, :]
    cos, sin = jnp.cos(ang), jnp.sin(ang)
    scale = (B_NOPE + B_ROPE) ** -0.5 * math.log2(math.e)
    zr = jnp.zeros((seq, LANES - B_NOPE - B_ROPE), F32)
    cosq = scale * jnp.concatenate([jnp.ones((seq, B_NOPE), F32), cos, cos, zr], axis=1)
    sinq = scale * jnp.concatenate([jnp.zeros((seq, B_NOPE), F32), -sin, sin, zr], axis=1)
    zk = jnp.zeros((seq, LANES - B_ROPE), F32)
    cosk = jnp.concatenate([cos, cos, zk], axis=1)
    sink = jnp.concatenate([-sin, sin, zk], axis=1)
    return w1, wqm, wqs, wk, wv, e, cosq, sinq, cosk, sink


def _router_weights(w_r1, b_r1, w_r2, b_r2):
    d = w_r1.shape[0]
    w2 = jnp.transpose(w_r2, (1, 0, 2)).reshape(d, N_EXPERTS)
    padw = jnp.zeros((d, LANES - N_EXPERTS - N_GROUPS), F32)
    wr = jnp.concatenate([w2, w_r1, padw], axis=1)
    wr_hi = wr.astype(BF16)
    wr_lo = (wr - wr_hi.astype(F32)).astype(BF16)
    br = jnp.concatenate([b_r2.reshape(N_EXPERTS), b_r1, jnp.zeros((LANES - N_EXPERTS - N_GROUPS,), F32)])
    return jnp.concatenate([wr_hi, wr_lo], axis=1), wr_hi, br[None, :]


def _layer(h2, p2, batch, seq, g_mix, w_in, g_cq, w_uq, g_ckv, w_ukv, g_out_a, g_out_b, w_o, g_ffn,
           w_r1, b_r1, w_r2, b_r2, w_e_gate, w_e_up, w_e_down, g_ple, w_ple_gate, w_ple_proj, g_out):
    t, d = h2.shape
    assert d == SUBLANES * LANES, "a token's activations must fill exactly one (8, 128) tile"
    w1, wqm, wqs, wk, wv, e, cosq, sinq, cosk, sink = _layer_weights(w_in, w_uq, w_ukv, seq)
    qkv, qf, kf, vb = _in_proj(h2, g_mix[None], w1, g_cq[None], wqm, wqs, g_ckv[None], wk, e, wv,
                               cosq, sinq, cosk, sink, seq)

    slopes = jnp.exp2(-8.0 * (jnp.arange(A_HEADS, dtype=F32) + 1.0) / A_HEADS)
    for window, dil in DILATED_BRANCHES:
        assert window // (2 * dil) == BAND_HALF
    bias = jnp.stack([_band_bias(slopes * dil) for _, dil in DILATED_BRANCHES])
    oa = _dilated(qkv.reshape(batch, seq, -1), bias, batch, seq).reshape(t, A_WIDTH)
    ob = _mla(qf, kf, vb, batch, seq).reshape(t, B_WIDTH)

    wr2, wrh, br = _router_weights(w_r1, b_r1, w_r2, b_r2)
    tri = (jnp.arange(TOKEN_TILE)[:, None] > jnp.arange(TOKEN_TILE)[None, :]).astype(BF16)
    h1, rec, route, cnt = _post_mix(
        h2, oa, ob, g_out_a[None], g_out_b[None], w_o[:A_WIDTH].astype(BF16), w_o[A_WIDTH:].astype(BF16),
        g_ffn[None], wr2, wrh, br, tri)

    counts = cnt[0].astype(jnp.int32)
    tile_ids, keys, valid, first, offs = _work_items(counts, t)
    pos = offs[route[:, 0].astype(jnp.int32)] + route[:, 1].astype(jnp.int32)
    xs = _dispatch(pos, rec)
    ys = _experts(tile_ids, keys, valid, first, offs, xs,
                  w_e_gate.astype(BF16), w_e_up.astype(BF16), w_e_down.astype(BF16))
    return _finish(pos, ys, h1, p2, g_ple[None], w_ple_gate.astype(BF16),
                   w_ple_proj.astype(BF16), g_out[None])


def kernel(x, p, g_mix, w_in, g_cq, w_uq, g_ckv, w_ukv, g_out_a, g_out_b, w_o, g_ffn, w_r1, b_r1, w_r2,
           b_r2, w_e_gate, w_e_up, w_e_down, g_ple, w_ple_gate, w_ple_proj, g_final):
    batch, seq, d = x.shape
    depth = p.shape[0]
    assert depth == 1, "the final norm is fused into the single layer's last kernel"
    h = x.reshape(batch * seq, d)
    i = 0
    h = _layer(h, p[i].reshape(batch * seq, -1), batch, seq, g_mix[i], w_in[i], g_cq[i], w_uq[i],
               g_ckv[i], w_ukv[i], g_out_a[i], g_out_b[i], w_o[i], g_ffn[i], w_r1[i], b_r1[i], w_r2[i],
               b_r2[i], w_e_gate[i], w_e_up[i], w_e_down[i], g_ple[i], w_ple_gate[i], w_ple_proj[i],
               g_final)
    return h.reshape(batch, seq, d)
```

```python
import functools
import math

import jax
import jax.numpy as jnp
from jax import lax
from jax.experimental import pallas as pl
from jax.experimental.pallas import tpu as pltpu

F32 = jnp.float32
BF16 = jnp.bfloat16

EPS = 1e-6
MASK_VALUE = -1e30
LANES = 128
SUBLANES = 8

A_HEADS = 8
A_HEAD_DIM = 64
A_WIDTH = A_HEADS * A_HEAD_DIM
DILATED_BRANCHES = ((128, 1), (512, 4), (2048, 16))

B_HEADS = 8
B_NOPE = 64
B_ROPE = 32
B_VDIM = 64
B_WIDTH = B_HEADS * B_VDIM
ROPE_THETA = 10000.0

N_GROUPS = 4
EXPERTS_PER_GROUP = 8
N_EXPERTS = N_GROUPS * EXPERTS_PER_GROUP
N_PAIR_KEYS = N_EXPERTS * EXPERTS_PER_GROUP
N_PAIRS = N_GROUPS * EXPERTS_PER_GROUP * (EXPERTS_PER_GROUP - 1) // 2

VMEM_LIMIT = 56 * 1024 * 1024

TOKEN_TILE = 256
BAND_TILE = 128
BAND_HALF = 64
BAND_KEYS = 256
BAND_UNROLL = 8
MERGE_ROWS = 256
MLA_TQ = 256
EXPERT_ROWS = 256
REC_ROWS = 2 * SUBLANES
GATE_ROW = SUBLANES
DMA_ISSUE_UNROLL = 8
ROUTE_GATE_LO, ROUTE_GATE_HI, ROUTE_KEY, ROUTE_RANK = 0, 1, 2, 3


def _rms(x, g):
    return x * lax.rsqrt(jnp.mean(x * x, axis=-1, keepdims=True) + EPS) * g


def _params(*sem):
    return pltpu.CompilerParams(dimension_semantics=sem, vmem_limit_bytes=VMEM_LIMIT)


def _token_cols(ref, n_tokens, rows_per_token, lead=None):
    def rows(c):
        sl = pl.ds(c, n_tokens, stride=rows_per_token)
        return ref[sl, :] if lead is None else ref[lead, sl, :]
    return jnp.concatenate([rows(c) for c in range(SUBLANES)], axis=1)


def _in_proj_kernel(x_ref, gmix_ref, w1_ref, gcq_ref, wqm_ref, wqs_ref, gckv_ref, wk_ref,
                    e_ref, wv_ref, cosq_ref, sinq_ref, cosk_ref, sink_ref,
                    qkv_ref, qf_ref, kf_ref, vb_ref, *, dq, dkv):
    a = _rms(x_ref[...], gmix_ref[...]).astype(BF16)
    proj = jnp.dot(a, w1_ref[...], preferred_element_type=F32)
    o = 3 * A_WIDTH
    qkv_ref[...] = proj[:, :o]
    cq = _rms(proj[:, o:o + dq], gcq_ref[...]).astype(BF16)
    o += dq
    ckv = _rms(proj[:, o:o + dkv], gckv_ref[...]).astype(BF16)
    o += dkv
    kr = proj[:, o:o + LANES]
    kr_sw = proj[:, o + LANES:o + 2 * LANES]

    qm = jnp.dot(cq, wqm_ref[...], preferred_element_type=F32)
    qs = jnp.dot(cq, wqs_ref[...], preferred_element_type=F32)
    cosq = cosq_ref[...]
    sinq = sinq_ref[...]
    for h in range(B_HEADS):
        sl = slice(h * LANES, (h + 1) * LANES)
        qf_ref[:, sl] = (qm[:, sl] * cosq + qs[:, sl] * sinq).astype(BF16)

    krot = (kr * cosk_ref[...] + kr_sw * sink_ref[...]).astype(BF16)
    kf = jnp.dot(ckv, wk_ref[...], preferred_element_type=F32)
    kf = kf + jnp.dot(krot, e_ref[...], preferred_element_type=F32)
    kf_ref[...] = kf.astype(BF16)
    vb_ref[...] = jnp.dot(ckv, wv_ref[...], preferred_element_type=F32).astype(BF16)


def _in_proj(x2, gmix, w1, gcq, wqm, wqs, gckv, wk, e, wv, cosq, sinq, cosk, sink, seq):
    t, d = x2.shape
    tm = TOKEN_TILE
    n_pos = seq // tm
    dq, dkv = wqm.shape[0], wk.shape[0]
    full = lambda arr: pl.BlockSpec(arr.shape, lambda i: (0, 0))
    tab = pl.BlockSpec((tm, LANES), lambda i: (i % n_pos, 0))
    row = lambda w: pl.BlockSpec((tm, w), lambda i: (i, 0))
    outs = [(3 * A_WIDTH, F32), (B_HEADS * LANES, BF16), (B_HEADS * LANES, BF16), (B_WIDTH, BF16)]
    return pl.pallas_call(
        functools.partial(_in_proj_kernel, dq=dq, dkv=dkv),
        grid=(t // tm,),
        in_specs=[row(d), full(gmix), full(w1), full(gcq), full(wqm), full(wqs), full(gckv),
                  full(wk), full(e), full(wv), tab, tab, tab, tab],
        out_specs=[row(w) for w, _ in outs],
        out_shape=[jax.ShapeDtypeStruct((t, w), dt) for w, dt in outs],
        compiler_params=_params("parallel"),
        name="in_proj",
    )(x2, gmix, w1, gcq, wqm, wqs, gckv, wk, e, wv, cosq, sinq, cosk, sink)


def _band_bias(slopes_scaled):
    qi = jnp.arange(BAND_TILE)[:, None]
    kc = jnp.arange(BAND_KEYS)[None, :]
    variants = []
    for shift in (0, BAND_HALF, BAND_KEYS - BAND_TILE):
        rel = jnp.abs(kc - shift - qi).astype(F32)
        bias = -slopes_scaled[:, None, None] * rel[None]
        variants.append(jnp.where(rel[None] <= BAND_HALF, bias, MASK_VALUE))
    return jnp.stack(variants, axis=1)


def _dilated_kernel(q_ref, k_ref, v_ref, bias_ref, out_ref, o_scr, lse_scr, *, seq):
    lane = lax.broadcasted_iota(jnp.int32, (BAND_TILE, LANES), 1)
    low_half = lane < A_HEAD_DIM

    for n, (_, dil) in enumerate(DILATED_BRANCHES):
        cls_len = seq // dil
        tiles_per_class = cls_len // BAND_TILE

        def tile(it, carry, n=n, dil=dil, cls_len=cls_len, tiles_per_class=tiles_per_class):
            r = it // tiles_per_class
            l0 = (it % tiles_per_class) * BAND_TILE
            start = jnp.clip(l0 - BAND_HALF, 0, cls_len - BAND_KEYS)
            variant = jnp.where(l0 == 0, 0, jnp.where(l0 == cls_len - BAND_TILE, 2, 1))
            if dil == 1:
                q_rows = pl.ds(pl.multiple_of(l0, BAND_TILE), BAND_TILE)
                k_rows = pl.ds(pl.multiple_of(start, BAND_HALF), BAND_KEYS)
            else:
                q_rows = pl.ds(r + dil * l0, BAND_TILE, stride=dil)
                k_rows = pl.ds(r + dil * start, BAND_KEYS, stride=dil)
            qp = q_ref[q_rows, :].astype(BF16)
            kp = k_ref[k_rows, :].astype(BF16)
            vp = v_ref[k_rows, :].astype(BF16)
            outs, lses = [], []
            for hh in range(2):
                qh = jnp.where(low_half if hh == 0 else ~low_half, qp, jnp.zeros_like(qp))
                s = lax.dot_general(qh, kp, (((1,), (1,)), ((), ())), preferred_element_type=F32)
                s = s + bias_ref[n, hh, variant]
                m = jnp.max(s, axis=-1, keepdims=True)
                p = jnp.exp(s - m)
                den = jnp.sum(p, axis=-1, keepdims=True)
                pv = jnp.dot(p.astype(BF16), vp, preferred_element_type=F32)
                outs.append(pv / den)
                lses.append(m + jnp.log(den))
            o_scr[n, q_rows, :] = jnp.where(low_half, outs[0], outs[1])
            lse_scr[n, q_rows, :] = jnp.where(low_half, lses[0], lses[1])
            return carry

        lax.fori_loop(0, seq // BAND_TILE, tile, 0, unroll=BAND_UNROLL)

    def merge(c, carry):
        rows = pl.ds(pl.multiple_of(c * MERGE_ROWS, MERGE_ROWS), MERGE_ROWS)
        lses = [lse_scr[n, rows, :] for n in range(len(DILATED_BRANCHES))]
        top = functools.reduce(jnp.maximum, lses)
        es = [jnp.exp(l - top) for l in lses]
        acc = sum(e * o_scr[n, rows, :] for n, e in enumerate(es))
        out_ref[rows, :] = (acc / sum(es)).astype(BF16)
        return carry

    lax.fori_loop(0, seq // MERGE_ROWS, merge, 0)


def _dilated(qkv, bias, batch, seq):
    for _, dil in DILATED_BRANCHES:
        assert seq // dil >= BAND_KEYS and (seq // dil) % BAND_TILE == 0
    pairs = A_HEADS // 2
    n_br = len(DILATED_BRANCHES)
    col = lambda base: pl.BlockSpec((None, seq, LANES), lambda b, hp: (b, 0, base + hp))
    return pl.pallas_call(
        functools.partial(_dilated_kernel, seq=seq),
        grid=(batch, pairs),
        in_specs=[col(0), col(pairs), col(2 * pairs),
                  pl.BlockSpec((n_br, 2, 3, BAND_TILE, BAND_KEYS), lambda b, hp: (0, hp, 0, 0, 0))],
        out_specs=col(0),
        out_shape=jax.ShapeDtypeStruct((batch, seq, A_WIDTH), BF16),
        scratch_shapes=[pltpu.VMEM((n_br, seq, LANES), F32), pltpu.VMEM((n_br, seq, LANES), F32)],
        compiler_params=_params("parallel", "parallel"),
        name="dilated",
    )(qkv, qkv, qkv, bias)


def _mla_kernel(q_ref, k_ref, v_ref, o_ref):
    lane = lax.broadcasted_iota(jnp.int32, (MLA_TQ, LANES), 1)
    scores = []
    for hh in range(2):
        sl = slice(hh * LANES, (hh + 1) * LANES)
        scores.append(lax.dot_general(q_ref[:, sl], k_ref[:, sl], (((1,), (1,)), ((), ())),
                                      preferred_element_type=F32))
    outs = []
    for s in scores:
        p = jnp.exp2(s - jnp.max(s, axis=-1, keepdims=True))
        den = jnp.sum(p, axis=-1, keepdims=True)
        pv = jnp.dot(p.astype(BF16), v_ref[...], preferred_element_type=F32)
        outs.append(pv / den)
    o_ref[...] = jnp.where(lane < B_VDIM, outs[0], outs[1]).astype(BF16)


def _mla(qf, kf, vb, batch, seq):
    pairs = B_HEADS // 2
    return pl.pallas_call(
        _mla_kernel,
        grid=(batch, pairs, seq // MLA_TQ),
        in_specs=[pl.BlockSpec((None, MLA_TQ, 2 * LANES), lambda b, hp, i: (b, i, hp)),
                  pl.BlockSpec((None, seq, 2 * LANES), lambda b, hp, i: (b, 0, hp)),
                  pl.BlockSpec((None, seq, LANES), lambda b, hp, i: (b, 0, hp))],
        out_specs=pl.BlockSpec((None, MLA_TQ, LANES), lambda b, hp, i: (b, i, hp)),
        out_shape=jax.ShapeDtypeStruct((batch, seq, B_WIDTH), BF16),
        compiler_params=_params("parallel", "parallel", "parallel"),
        name="mla",
    )(qf.reshape(batch, seq, -1), kf.reshape(batch, seq, -1), vb.reshape(batch, seq, -1))


def _post_mix_kernel(x_ref, oa_ref, ob_ref, ga_ref, gb_ref, woa_ref, wob_ref, gffn_ref,
                     wr2_ref, wrh_ref, br_ref, tri_ref,
                     h_ref, route_ref, cnt_ref, carry_ref):
    i = pl.program_id(0)

    @pl.when(i == 0)
    def _():
        carry_ref[...] = jnp.zeros_like(carry_ref)

    na = _rms(oa_ref[...].astype(F32), ga_ref[...]).astype(BF16)
    nb = _rms(ob_ref[...].astype(F32), gb_ref[...]).astype(BF16)
    h = x_ref[...] + jnp.dot(na, woa_ref[...], preferred_element_type=F32)
    h = h + jnp.dot(nb, wob_ref[...], preferred_element_type=F32)
    h_ref[...] = h
    m = _rms(h, gffn_ref[...])
    tm = m.shape[0]

    m_hi = m.astype(BF16)
    m_lo = (m - m_hi.astype(F32)).astype(BF16)
    part = jnp.dot(m_hi, wr2_ref[...], preferred_element_type=F32)
    logits = part[:, :LANES] + part[:, LANES:] + br_ref[...]
    logits = logits + jnp.dot(m_lo, wrh_ref[...], preferred_element_type=F32)

    lane_i = lax.broadcasted_iota(jnp.int32, (tm, LANES), 1)
    lane = lane_i.astype(F32)
    big = float(LANES)
    is_group = (lane_i >= N_EXPERTS) & (lane_i < N_EXPERTS + N_GROUPS)
    lg = jnp.where(is_group, logits, -jnp.inf)
    g_max = jnp.max(lg, axis=-1, keepdims=True)
    g_idx = jnp.min(jnp.where(lg == g_max, lane - N_EXPERTS, big), axis=-1, keepdims=True)
    g_val = 1.0 / jnp.sum(jnp.exp(lg - g_max), axis=-1, keepdims=True)

    group_of_lane = (lane_i // EXPERTS_PER_GROUP).astype(F32)
    in_group = (lane_i < N_EXPERTS) & (group_of_lane == g_idx)
    le = jnp.where(in_group, logits, -jnp.inf)
    e_max = jnp.max(le, axis=-1, keepdims=True)
    ee = jnp.exp(le - e_max)
    p2 = jnp.where(in_group, ee / jnp.sum(ee, axis=-1, keepdims=True), -1.0)
    v1 = jnp.max(p2, axis=-1, keepdims=True)
    i1 = jnp.min(jnp.where(p2 == v1, lane, big), axis=-1, keepdims=True)
    p2b = jnp.where(lane == i1, -1.0, p2)
    v2 = jnp.max(p2b, axis=-1, keepdims=True)
    i2 = jnp.min(jnp.where(p2b == v2, lane, big), axis=-1, keepdims=True)
    norm = g_val / (v1 + v2)
    gate1 = v1 * norm
    gate2 = v2 * norm

    first_is_lo = i1 < i2
    e_lo = jnp.where(first_is_lo, i1, i2)
    e_hi = jnp.where(first_is_lo, i2, i1)
    g_lo = jnp.where(first_is_lo, gate1, gate2)
    g_hi = jnp.where(first_is_lo, gate2, gate1)
    key = e_lo * EXPERTS_PER_GROUP + (e_hi - g_idx * EXPERTS_PER_GROUP)

    key_lane = lax.broadcasted_iota(jnp.int32, (tm, N_PAIR_KEYS), 1).astype(F32)
    onehot = key_lane == key
    prefix = jnp.dot(tri_ref[...], onehot.astype(BF16), preferred_element_type=F32)
    carry = carry_ref[...]
    rank = jnp.sum(jnp.where(onehot, carry + prefix, 0.0), axis=-1, keepdims=True)
    carry = carry + jnp.sum(onehot.astype(F32), axis=0, keepdims=True)
    carry_ref[...] = carry
    cnt_ref[...] = jnp.broadcast_to(carry, cnt_ref.shape)

    route = jnp.where(lane_i == ROUTE_GATE_LO, g_lo, jnp.where(lane_i == ROUTE_GATE_HI, g_hi, 0.0))
    route_ref[...] = jnp.where(lane_i == ROUTE_KEY, key, jnp.where(lane_i == ROUTE_RANK, rank, route))


def _post_mix(x2, oa, ob, ga, gb, woa, wob, gffn, wr2, wrh, br, tri):
    t, d = x2.shape
    tm = TOKEN_TILE
    full = lambda arr: pl.BlockSpec(arr.shape, lambda i: (0, 0))
    row = lambda w: pl.BlockSpec((tm, w), lambda i: (i, 0))
    return pl.pallas_call(
        _post_mix_kernel,
        grid=(t // tm,),
        in_specs=[row(d), row(A_WIDTH), row(B_WIDTH), full(ga), full(gb), full(woa), full(wob),
                  full(gffn), full(wr2), full(wrh), full(br), full(tri)],
        out_specs=[row(d), row(LANES), pl.BlockSpec((SUBLANES, N_PAIR_KEYS), lambda i: (0, 0))],
        out_shape=[jax.ShapeDtypeStruct((t, d), F32),
                   jax.ShapeDtypeStruct((t, LANES), F32),
                   jax.ShapeDtypeStruct((SUBLANES, N_PAIR_KEYS), F32)],
        scratch_shapes=[pltpu.VMEM((1, N_PAIR_KEYS), F32)],
        compiler_params=_params("arbitrary"),
        name="post_mix",
    )(x2, oa, ob, ga, gb, woa, wob, gffn, wr2, wrh, br, tri)


def _rows_copy(src, src_token, dst, dst_token, rows_per_token, sem):
    src_rows = pl.ds(pl.multiple_of(src_token * rows_per_token, rows_per_token), rows_per_token)
    dst_rows = pl.ds(pl.multiple_of(dst_token * rows_per_token, rows_per_token), rows_per_token)
    return pltpu.make_async_copy(src.at[src_rows, :], dst.at[dst_rows, :], sem)


def _wait_all_rows(vmem_buf, hbm_ref, n_rows, sem, *, to_hbm):
    hbm_rows = hbm_ref.at[pl.ds(0, n_rows), :]
    src, dst = (vmem_buf, hbm_rows) if to_hbm else (hbm_rows, vmem_buf)
    pltpu.make_async_copy(src, dst, sem).wait()


def _with_static_slot(slot, body):
    for s in range(2):
        @pl.when(slot == s)
        def _(s=s):
            body(s)


def _dispatch_kernel(offs_ref, key_ref, rank_ref, h_ref, route_ref, gffn_ref,
                     xs_ref, pos_ref, rec_ref, sems, *, n_steps):
    i = pl.program_id(0)
    tm = h_ref.shape[0]

    def drain(s):
        _wait_all_rows(rec_ref.at[s], xs_ref, tm * REC_ROWS, sems.at[s], to_hbm=True)

    def step(s):
        @pl.when(i >= 2)
        def _():
            drain(s)

        @pl.when(i < 2)
        def _():
            rec_ref[s] = jnp.zeros(rec_ref.shape[1:], F32)

        m = _rms(h_ref[...], gffn_ref[...])
        for c in range(SUBLANES):
            rec_ref[s, pl.ds(c, tm, stride=REC_ROWS), :] = m[:, c * LANES:(c + 1) * LANES]
        rec_ref[s, pl.ds(GATE_ROW, tm, stride=REC_ROWS), :] = route_ref[...]

        def issue(j, carry):
            pos = offs_ref[key_ref[j]] + rank_ref[j]
            pos_ref[j] = pos
            _rows_copy(rec_ref.at[s], j, xs_ref, pos, REC_ROWS, sems.at[s]).start()
            return carry

        lax.fori_loop(0, tm, issue, 0, unroll=DMA_ISSUE_UNROLL)

        @pl.when(i == n_steps - 1)
        def _():
            drain(s)
            if n_steps > 1:
                drain(1 - s)

    _with_static_slot(i % 2, step)


def _dispatch(offs, key, rank, h, route, gffn):
    t, d = h.shape
    tm = TOKEN_TILE
    n_steps = t // tm
    grid_spec = pltpu.PrefetchScalarGridSpec(
        num_scalar_prefetch=1,
        grid=(n_steps,),
        in_specs=[pl.BlockSpec((tm,), lambda i, of: (i,), memory_space=pltpu.SMEM),
                  pl.BlockSpec((tm,), lambda i, of: (i,), memory_space=pltpu.SMEM),
                  pl.BlockSpec((tm, d), lambda i, of: (i, 0)),
                  pl.BlockSpec((tm, LANES), lambda i, of: (i, 0)),
                  pl.BlockSpec(gffn.shape, lambda i, of: (0, 0))],
        out_specs=[pl.BlockSpec(memory_space=pl.ANY),
                   pl.BlockSpec((tm,), lambda i, of: (i,), memory_space=pltpu.SMEM)],
        scratch_shapes=[pltpu.VMEM((2, tm * REC_ROWS, LANES), F32), pltpu.SemaphoreType.DMA((2,))],
    )
    return pl.pallas_call(
        functools.partial(_dispatch_kernel, n_steps=n_steps),
        grid_spec=grid_spec,
        out_shape=[jax.ShapeDtypeStruct((t * REC_ROWS, LANES), F32), jax.ShapeDtypeStruct((t,), jnp.int32)],
        compiler_params=_params("arbitrary"),
        name="dispatch",
    )(offs, key, rank, h, route, gffn)


def _ffn(x, wg_ref, wu_ref, wd_ref):
    g = jnp.dot(x, wg_ref[0], preferred_element_type=F32)
    u = jnp.dot(x, wu_ref[0], preferred_element_type=F32)
    hid = (g * (1.0 / (1.0 + jnp.exp(-g))) * u).astype(BF16)
    return jnp.dot(hid, wd_ref[0], preferred_element_type=F32)


def _experts_kernel(tile_ref, key_ref, valid_ref, first_ref, last_ref, offs_ref,
                    xs_ref, wg_lo, wu_lo, wd_lo, wg_hi, wu_hi, wd_hi, ys_ref, x_scr, gate_scr, y_scr):
    w = pl.program_id(0)
    rows_n = EXPERT_ROWS

    @pl.when(valid_ref[w] == 1)
    def _():
        is_first = first_ref[w] == 1

        @pl.when(is_first)
        def _():
            x_scr[...] = _token_cols(xs_ref, rows_n, REC_ROWS).astype(BF16)
            gate_scr[...] = xs_ref[pl.ds(GATE_ROW, rows_n, stride=REC_ROWS), :]

        x = x_scr[...]
        gates = gate_scr[...]
        y = (gates[:, ROUTE_GATE_LO:ROUTE_GATE_LO + 1] * _ffn(x, wg_lo, wu_lo, wd_lo)
             + gates[:, ROUTE_GATE_HI:ROUTE_GATE_HI + 1] * _ffn(x, wg_hi, wu_hi, wd_hi))
        key = key_ref[w]
        rows = tile_ref[w] * rows_n + lax.broadcasted_iota(jnp.int32, (rows_n, 1), 0)
        mine = (rows >= offs_ref[key]) & (rows < offs_ref[key + 1])

        @pl.when(is_first)
        def _():
            y_scr[...] = jnp.where(mine, y, 0.0)

        @pl.when(jnp.logical_not(is_first))
        def _():
            y_scr[...] = jnp.where(mine, y, y_scr[...])

        @pl.when(last_ref[w] == 1)
        def _():
            for c in range(SUBLANES):
                ys_ref[pl.ds(c, rows_n, stride=SUBLANES), :] = y_scr[:, c * LANES:(c + 1) * LANES]


def _experts(tile_ids, keys, valid, first, last, offs, xs, wg, wu, wd):
    n_tok = xs.shape[0] // REC_ROWS
    d, ff = wg.shape[1], wg.shape[2]
    n_items = tile_ids.shape[0]
    lo = lambda w, ti, ke, va, fi, la, of: (ke[w] // EXPERTS_PER_GROUP, 0, 0)
    hi = lambda w, ti, ke, va, fi, la, of: (
        ke[w] // (EXPERTS_PER_GROUP * EXPERTS_PER_GROUP) * EXPERTS_PER_GROUP + ke[w] % EXPERTS_PER_GROUP, 0, 0)
    tile = lambda w, ti, ke, va, fi, la, of: (ti[w], 0)
    grid_spec = pltpu.PrefetchScalarGridSpec(
        num_scalar_prefetch=6,
        grid=(n_items,),
        in_specs=[pl.BlockSpec((EXPERT_ROWS * REC_ROWS, LANES), tile),
                  pl.BlockSpec((1, d, ff), lo), pl.BlockSpec((1, d, ff), lo), pl.BlockSpec((1, ff, d), lo),
                  pl.BlockSpec((1, d, ff), hi), pl.BlockSpec((1, d, ff), hi), pl.BlockSpec((1, ff, d), hi)],
        out_specs=pl.BlockSpec((EXPERT_ROWS * SUBLANES, LANES), tile),
        scratch_shapes=[pltpu.VMEM((EXPERT_ROWS, d), BF16), pltpu.VMEM((EXPERT_ROWS, LANES), F32),
                        pltpu.VMEM((EXPERT_ROWS, d), F32)],
    )
    return pl.pallas_call(
        _experts_kernel,
        grid_spec=grid_spec,
        out_shape=jax.ShapeDtypeStruct((n_tok * SUBLANES, LANES), F32),
        compiler_params=_params("arbitrary"),
        name="experts",
    )(tile_ids, keys, valid, first, last, offs, xs, wg, wu, wd, wg, wu, wd)


def _work_items(counts, n_rows):
    n_keys = counts.shape[0]
    n_tiles = n_rows // EXPERT_ROWS
    n_items = n_tiles + N_PAIRS - 1
    offs = jnp.concatenate([jnp.zeros((1,), jnp.int32), jnp.cumsum(counts)])
    first_tile = offs[:-1] // EXPERT_ROWS
    last_tile = (offs[1:] - 1) // EXPERT_ROWS
    per_key = jnp.where(counts > 0, last_tile - first_tile + 1, 0)
    item_end = jnp.cumsum(per_key)
    item_start = item_end - per_key
    w = jnp.arange(n_items, dtype=jnp.int32)
    valid = w < item_end[-1]
    key = jnp.sum((item_end[None, :] <= w[:, None]).astype(jnp.int32), axis=1)
    key = jnp.minimum(key, n_keys - 1)
    tile = first_tile[key] + (w - item_start[key])
    last_valid = jnp.maximum(item_end[-1] - 1, 0)
    tile = jnp.where(valid, tile, tile[last_valid]).astype(jnp.int32)
    key = jnp.where(valid, key, key[last_valid])
    changes = tile[1:] != tile[:-1]
    first = jnp.concatenate([jnp.ones((1,), bool), changes])
    last = jnp.concatenate([changes, jnp.ones((1,), bool)]) | (w == last_valid)
    as_i32 = lambda a: a.astype(jnp.int32)
    return tile, key, as_i32(valid), as_i32(first), as_i32(last), as_i32(offs)


def _finish_kernel(pos_ref, pos_next_ref, ys_ref, h_ref, p_ref, gple_ref, wpg_ref, wpp_ref, gfin_ref,
                   out_ref, rows_ref, sems, *, n_steps):
    i = pl.program_id(0)
    tm = h_ref.shape[0]

    def gather(positions, s):
        def body(j, carry):
            _rows_copy(ys_ref, positions[j], rows_ref.at[s], j, SUBLANES, sems.at[s]).start()
            return carry
        lax.fori_loop(0, tm, body, 0, unroll=DMA_ISSUE_UNROLL)

    def step(s):
        @pl.when(i == 0)
        def _():
            gather(pos_ref, s)

        @pl.when(i + 1 < n_steps)
        def _():
            gather(pos_next_ref, 1 - s)

        _wait_all_rows(rows_ref.at[s], ys_ref, tm * SUBLANES, sems.at[s], to_hbm=False)
        h = h_ref[...] + _token_cols(rows_ref, tm, SUBLANES, lead=s)
        gate_in = _rms(h, gple_ref[...]).astype(BF16)
        z = jnp.dot(gate_in, wpg_ref[...], preferred_element_type=F32)
        gate = 1.0 / (1.0 + jnp.exp(-z))
        h = h + gate * jnp.dot(p_ref[...].astype(BF16), wpp_ref[...], preferred_element_type=F32)
        out_ref[...] = _rms(h, gfin_ref[...])

    _with_static_slot(i % 2, step)


def _finish(pos, ys, h, p2, gple, wpg, wpp, gfin):
    t, d = h.shape
    tm = TOKEN_TILE
    full = lambda arr: pl.BlockSpec(arr.shape, lambda i: (0, 0))
    row = lambda w: pl.BlockSpec((tm, w), lambda i: (i, 0))
    n_steps = t // tm
    return pl.pallas_call(
        functools.partial(_finish_kernel, n_steps=n_steps),
        grid=(n_steps,),
        in_specs=[pl.BlockSpec((tm,), lambda i: (i,), memory_space=pltpu.SMEM),
                  pl.BlockSpec((tm,), lambda i: (jnp.minimum(i + 1, n_steps - 1),), memory_space=pltpu.SMEM),
                  pl.BlockSpec(memory_space=pl.ANY), row(d), row(p2.shape[1]),
                  full(gple), full(wpg), full(wpp), full(gfin)],
        out_specs=row(d),
        out_shape=jax.ShapeDtypeStruct((t, d), F32),
        scratch_shapes=[pltpu.VMEM((2, tm * SUBLANES, LANES), F32), pltpu.SemaphoreType.DMA((2,))],
        compiler_params=_params("arbitrary"),
        name="finish",
    )(pos, pos, ys, h, p2, gple, wpg, wpp, gfin)


def _swap_halves(w):
    half = w.shape[-1] // 2
    return jnp.concatenate([w[..., half:], w[..., :half]], axis=-1)


def _layer_weights(w_in, w_uq, w_ukv, seq):
    d = w_in.shape[0]
    dq, dkv = w_uq.shape[0], w_ukv.shape[0]
    o = 3 * A_WIDTH + dq + dkv
    w_kr = w_in[:, o:o + B_ROPE]
    pad = jnp.zeros((d, LANES - B_ROPE), F32)
    w_q = w_in[:, :A_WIDTH] * (A_HEAD_DIM ** -0.5)
    w1 = jnp.concatenate([w_q, w_in[:, A_WIDTH:o], w_kr, pad, _swap_halves(w_kr), pad], axis=1).astype(BF16)

    uq = w_uq.reshape(dq, B_HEADS, B_NOPE + B_ROPE)
    zq = jnp.zeros((dq, B_HEADS, LANES - B_NOPE - B_ROPE), F32)
    wqm = jnp.concatenate([uq, zq], axis=-1).reshape(dq, B_HEADS * LANES).astype(BF16)
    wqs = jnp.concatenate([jnp.zeros((dq, B_HEADS, B_NOPE), F32), _swap_halves(uq[..., B_NOPE:]), zq],
                          axis=-1).reshape(dq, B_HEADS * LANES).astype(BF16)

    ukv = w_ukv.reshape(dkv, B_HEADS, B_NOPE + B_VDIM)
    wk = jnp.concatenate([ukv[..., :B_NOPE], jnp.zeros((dkv, B_HEADS, LANES - B_NOPE), F32)],
                         axis=-1).reshape(dkv, B_HEADS * LANES).astype(BF16)
    wv = ukv[..., B_NOPE:].reshape(dkv, B_WIDTH).astype(BF16)

    j = jnp.arange(LANES)[:, None]
    col = jnp.arange(B_HEADS * LANES)[None, :]
    place = ((col % LANES) == j + B_NOPE) & (j < B_ROPE)
    e = place.astype(BF16)

    inv_freq = 1.0 / (ROPE_THETA ** (jnp.arange(0, B_ROPE, 2, dtype=F32) / B_ROPE))
    ang = jnp.arange(seq, dtype=F32)[:, None] * inv_freq[None, :]
    cos, sin = jnp.cos(ang), jnp.sin(ang)
    scale = (B_NOPE + B_ROPE) ** -0.5 * math.log2(math.e)
    zr = jnp.zeros((seq, LANES - B_NOPE - B_ROPE), F32)
    cosq = scale * jnp.concatenate([jnp.ones((seq, B_NOPE), F32), cos, cos, zr], axis=1)
    sinq = scale * jnp.concatenate([jnp.zeros((seq, B_NOPE), F32), -sin, sin, zr], axis=1)
    zk = jnp.zeros((seq, LANES - B_ROPE), F32)
    cosk = jnp.concatenate([cos, cos, zk], axis=1)
    sink = jnp.concatenate([-sin, sin, zk], axis=1)
    return w1, wqm, wqs, wk, wv, e, cosq, sinq, cosk, sink


def _router_weights(w_r1, b_r1, w_r2, b_r2):
    d = w_r1.shape[0]
    w2 = jnp.transpose(w_r2, (1, 0, 2)).reshape(d, N_EXPERTS)
    padw = jnp.zeros((d, LANES - N_EXPERTS - N_GROUPS), F32)
    wr = jnp.concatenate([w2, w_r1, padw], axis=1)
    wr_hi = wr.astype(BF16)
    wr_lo = (wr - wr_hi.astype(F32)).astype(BF16)
    br = jnp.concatenate([b_r2.reshape(N_EXPERTS), b_r1, jnp.zeros((LANES - N_EXPERTS - N_GROUPS,), F32)])
    return jnp.concatenate([wr_hi, wr_lo], axis=1), wr_hi, br[None, :]


def _layer(h2, p2, batch, seq, g_mix, w_in, g_cq, w_uq, g_ckv, w_ukv, g_out_a, g_out_b, w_o, g_ffn,
           w_r1, b_r1, w_r2, b_r2, w_e_gate, w_e_up, w_e_down, g_ple, w_ple_gate, w_ple_proj, g_out):
    t, d = h2.shape
    assert d == SUBLANES * LANES, "a token's activations must fill exactly one (8, 128) tile"
    w1, wqm, wqs, wk, wv, e, cosq, sinq, cosk, sink = _layer_weights(w_in, w_uq, w_ukv, seq)
    qkv, qf, kf, vb = _in_proj(h2, g_mix[None], w1, g_cq[None], wqm, wqs, g_ckv[None], wk, e, wv,
                               cosq, sinq, cosk, sink, seq)

    slopes = jnp.exp2(-8.0 * (jnp.arange(A_HEADS, dtype=F32) + 1.0) / A_HEADS)
    for window, dil in DILATED_BRANCHES:
        assert window // (2 * dil) == BAND_HALF
    bias = jnp.stack([_band_bias(slopes * dil) for _, dil in DILATED_BRANCHES])
    oa = _dilated(qkv.reshape(batch, seq, -1), bias, batch, seq).reshape(t, A_WIDTH)
    ob = _mla(qf, kf, vb, batch, seq).reshape(t, B_WIDTH)

    wr2, wrh, br = _router_weights(w_r1, b_r1, w_r2, b_r2)
    tri = (jnp.arange(TOKEN_TILE)[:, None] > jnp.arange(TOKEN_TILE)[None, :]).astype(BF16)
    h1, route, cnt = _post_mix(
        h2, oa, ob, g_out_a[None], g_out_b[None], w_o[:A_WIDTH].astype(BF16), w_o[A_WIDTH:].astype(BF16),
        g_ffn[None], wr2, wrh, br, tri)

    counts = cnt[0].astype(jnp.int32)
    tile_ids, keys, valid, first, last, offs = _work_items(counts, t)
    xs, pos = _dispatch(offs, route[:, ROUTE_KEY].astype(jnp.int32), route[:, ROUTE_RANK].astype(jnp.int32),
                        h1, route, g_ffn[None])
    ys = _experts(tile_ids, keys, valid, first, last, offs, xs,
                  w_e_gate.astype(BF16), w_e_up.astype(BF16), w_e_down.astype(BF16))
    return _finish(pos, ys, h1, p2, g_ple[None], w_ple_gate.astype(BF16),
                   w_ple_proj.astype(BF16), g_out[None])


def kernel(x, p, g_mix, w_in, g_cq, w_uq, g_ckv, w_ukv, g_out_a, g_out_b, w_o, g_ffn, w_r1, b_r1, w_r2,
           b_r2, w_e_gate, w_e_up, w_e_down, g_ple, w_ple_gate, w_ple_proj, g_final):
    batch, seq, d = x.shape
    depth = p.shape[0]
    assert depth == 1, "the final norm is fused into the single layer's last kernel"
    h = x.reshape(batch * seq, d)
    i = 0
    h = _layer(h, p[i].reshape(batch * seq, -1), batch, seq, g_mix[i], w_in[i], g_cq[i], w_uq[i],
               g_ckv[i], w_ukv[i], g_out_a[i], g_out_b[i], w_o[i], g_ffn[i], w_r1[i], b_r1[i], w_r2[i],
               b_r2[i], w_e_gate[i], w_e_up[i], w_e_down[i], g_ple[i], w_ple_gate[i], w_ple_proj[i],
               g_final)
    return h.reshape(batch, seq, d)
```

```python
import functools
import math

import jax
import jax.numpy as jnp
from jax import lax
from jax.experimental import pallas as pl
from jax.experimental.pallas import tpu as pltpu

F32 = jnp.float32
BF16 = jnp.bfloat16

EPS = 1e-6
MASK_VALUE = -1e30
LANES = 128
SUBLANES = 8

A_HEADS = 8
A_HEAD_DIM = 64
A_WIDTH = A_HEADS * A_HEAD_DIM
DILATED_BRANCHES = ((128, 1), (512, 4), (2048, 16))

B_HEADS = 8
B_NOPE = 64
B_ROPE = 32
B_VDIM = 64
B_WIDTH = B_HEADS * B_VDIM
ROPE_THETA = 10000.0

N_GROUPS = 4
EXPERTS_PER_GROUP = 8
N_EXPERTS = N_GROUPS * EXPERTS_PER_GROUP
N_PAIR_KEYS = N_EXPERTS * EXPERTS_PER_GROUP
N_PAIRS = N_GROUPS * EXPERTS_PER_GROUP * (EXPERTS_PER_GROUP - 1) // 2

VMEM_LIMIT = 56 * 1024 * 1024

PROJ_TILE = 512
TOKEN_TILE = 256
BAND_TILE = 128
BAND_HALF = 64
BAND_KEYS = 256
BAND_UNROLL = 16
MERGE_ROWS = 256
MLA_TQ = 256
EXPERT_ROWS = 256
REC_ROWS = 2 * SUBLANES
GATE_ROW = SUBLANES
DMA_ISSUE_UNROLL = 8
ROUTE_GATE_LO, ROUTE_GATE_HI, ROUTE_KEY, ROUTE_RANK = 0, 1, 2, 3


def _rms(x, g):
    return x * lax.rsqrt(jnp.mean(x * x, axis=-1, keepdims=True) + EPS) * g


def _params(*sem):
    return pltpu.CompilerParams(dimension_semantics=sem, vmem_limit_bytes=VMEM_LIMIT)


def _token_cols(ref, n_tokens, rows_per_token, lead=None):
    def rows(c):
        sl = pl.ds(c, n_tokens, stride=rows_per_token)
        return ref[sl, :] if lead is None else ref[lead, sl, :]
    return jnp.concatenate([rows(c) for c in range(SUBLANES)], axis=1)


def _swap_rope_halves(x, first_lane):
    half = B_ROPE // 2
    lane = lax.broadcasted_iota(jnp.int32, x.shape, 1)
    from_above = pltpu.roll(x, LANES - half, 1)
    from_below = pltpu.roll(x, half, 1)
    return jnp.where(lane < first_lane + half, from_above, from_below)


def _in_proj_kernel(x_ref, gmix_ref, w1_ref, gcq_ref, wqm_ref, gckv_ref, wk_ref, wv_ref,
                    cosq_ref, sinq_ref, cosk_ref, sink_ref,
                    qkv_ref, qf_ref, kf_ref, vb_ref, *, dq, dkv):
    a = _rms(x_ref[...], gmix_ref[...]).astype(BF16)
    o = 3 * A_WIDTH
    proj = jnp.dot(a, w1_ref[:, o:], preferred_element_type=F32)
    cq = _rms(proj[:, 0:dq], gcq_ref[...]).astype(BF16)
    ckv = _rms(proj[:, dq:dq + dkv], gckv_ref[...]).astype(BF16)
    kr = proj[:, dq + dkv:dq + dkv + LANES]

    qm = jnp.dot(cq, wqm_ref[...], preferred_element_type=F32)
    cosq = cosq_ref[...]
    sinq = sinq_ref[...]
    for h in range(B_HEADS):
        sl = slice(h * LANES, (h + 1) * LANES)
        qf_ref[:, sl] = (qm[:, sl] * cosq + _swap_rope_halves(qm[:, sl], B_NOPE) * sinq).astype(BF16)

    krot = kr * cosk_ref[...] + _swap_rope_halves(kr, 0) * sink_ref[...]
    krot = pltpu.roll(krot, B_NOPE, 1)
    kf = jnp.dot(ckv, wk_ref[...], preferred_element_type=F32)
    for h in range(B_HEADS):
        sl = slice(h * LANES, (h + 1) * LANES)
        kf_ref[:, sl] = (kf[:, sl] + krot).astype(BF16)
    vb_ref[...] = jnp.dot(ckv, wv_ref[...], preferred_element_type=F32).astype(BF16)
    qkv_ref[...] = jnp.dot(a, w1_ref[:, :o], preferred_element_type=F32)


def _in_proj(x2, gmix, w1, gcq, wqm, gckv, wk, wv, cosq, sinq, cosk, sink, seq):
    t, d = x2.shape
    tm = PROJ_TILE
    n_pos = seq // tm
    dq, dkv = wqm.shape[0], wk.shape[0]
    full = lambda arr: pl.BlockSpec(arr.shape, lambda i: (0, 0))
    tab = pl.BlockSpec((tm, LANES), lambda i: (i % n_pos, 0))
    row = lambda w: pl.BlockSpec((tm, w), lambda i: (i, 0))
    outs = [(3 * A_WIDTH, F32), (B_HEADS * LANES, BF16), (B_HEADS * LANES, BF16), (B_WIDTH, BF16)]
    return pl.pallas_call(
        functools.partial(_in_proj_kernel, dq=dq, dkv=dkv),
        grid=(t // tm,),
        in_specs=[row(d), full(gmix), full(w1), full(gcq), full(wqm), full(gckv),
                  full(wk), full(wv), tab, tab, tab, tab],
        out_specs=[row(w) for w, _ in outs],
        out_shape=[jax.ShapeDtypeStruct((t, w), dt) for w, dt in outs],
        compiler_params=_params("parallel"),
        name="in_proj",
    )(x2, gmix, w1, gcq, wqm, gckv, wk, wv, cosq, sinq, cosk, sink)


def _band_bias(slopes_scaled):
    qi = jnp.arange(BAND_TILE)[:, None]
    kc = jnp.arange(BAND_KEYS)[None, :]
    variants = []
    for shift in (0, BAND_HALF, BAND_KEYS - BAND_TILE):
        rel = jnp.abs(kc - shift - qi).astype(F32)
        bias = -slopes_scaled[:, None, None] * rel[None]
        variants.append(jnp.where(rel[None] <= BAND_HALF, bias, MASK_VALUE))
    return jnp.stack(variants, axis=1)


def _dilated_kernel(q_ref, k_ref, v_ref, bias_ref, out_ref, o_scr, lse_scr, *, seq):
    lane = lax.broadcasted_iota(jnp.int32, (BAND_TILE, LANES), 1)
    low_half = lane < A_HEAD_DIM

    for n, (_, dil) in enumerate(DILATED_BRANCHES):
        cls_len = seq // dil
        tiles_per_class = cls_len // BAND_TILE

        def tile(it, carry, n=n, dil=dil, cls_len=cls_len, tiles_per_class=tiles_per_class):
            r = it // tiles_per_class
            l0 = (it % tiles_per_class) * BAND_TILE
            start = jnp.clip(l0 - BAND_HALF, 0, cls_len - BAND_KEYS)
            variant = jnp.where(l0 == 0, 0, jnp.where(l0 == cls_len - BAND_TILE, 2, 1))
            if dil == 1:
                q_rows = pl.ds(pl.multiple_of(l0, BAND_TILE), BAND_TILE)
                k_rows = pl.ds(pl.multiple_of(start, BAND_HALF), BAND_KEYS)
            else:
                q_rows = pl.ds(r + dil * l0, BAND_TILE, stride=dil)
                k_rows = pl.ds(r + dil * start, BAND_KEYS, stride=dil)
            qp = q_ref[q_rows, :].astype(BF16)
            kp = k_ref[k_rows, :].astype(BF16)
            vp = v_ref[k_rows, :].astype(BF16)
            outs, lses = [], []
            for hh in range(2):
                qh = jnp.where(low_half if hh == 0 else ~low_half, qp, jnp.zeros_like(qp))
                s = lax.dot_general(qh, kp, (((1,), (1,)), ((), ())), preferred_element_type=F32)
                s = s + bias_ref[n, hh, variant]
                m = jnp.max(s, axis=-1, keepdims=True)
                p = jnp.exp(s - m)
                den = jnp.sum(p, axis=-1, keepdims=True)
                pv = jnp.dot(p.astype(BF16), vp, preferred_element_type=F32)
                outs.append(pv / den)
                lses.append(m + jnp.log(den))
            o_scr[n, q_rows, :] = jnp.where(low_half, outs[0], outs[1])
            lse_scr[n, q_rows, :] = jnp.where(low_half, lses[0], lses[1])
            return carry

        lax.fori_loop(0, seq // BAND_TILE, tile, 0, unroll=BAND_UNROLL)

    def merge(c, carry):
        rows = pl.ds(pl.multiple_of(c * MERGE_ROWS, MERGE_ROWS), MERGE_ROWS)
        lses = [lse_scr[n, rows, :] for n in range(len(DILATED_BRANCHES))]
        top = functools.reduce(jnp.maximum, lses)
        es = [jnp.exp(l - top) for l in lses]
        acc = sum(e * o_scr[n, rows, :] for n, e in enumerate(es))
        out_ref[rows, :] = (acc / sum(es)).astype(BF16)
        return carry

    lax.fori_loop(0, seq // MERGE_ROWS, merge, 0)


def _dilated(qkv, bias, batch, seq):
    for _, dil in DILATED_BRANCHES:
        assert seq // dil >= BAND_KEYS and (seq // dil) % BAND_TILE == 0
    pairs = A_HEADS // 2
    n_br = len(DILATED_BRANCHES)
    col = lambda base: pl.BlockSpec((None, seq, LANES), lambda b, hp: (b, 0, base + hp))
    return pl.pallas_call(
        functools.partial(_dilated_kernel, seq=seq),
        grid=(batch, pairs),
        in_specs=[col(0), col(pairs), col(2 * pairs),
                  pl.BlockSpec((n_br, 2, 3, BAND_TILE, BAND_KEYS), lambda b, hp: (0, hp, 0, 0, 0))],
        out_specs=col(0),
        out_shape=jax.ShapeDtypeStruct((batch, seq, A_WIDTH), BF16),
        scratch_shapes=[pltpu.VMEM((n_br, seq, LANES), F32), pltpu.VMEM((n_br, seq, LANES), F32)],
        compiler_params=_params("parallel", "parallel"),
        name="dilated",
    )(qkv, qkv, qkv, bias)


def _mla_kernel(q_ref, k_ref, v_ref, o_ref):
    lane = lax.broadcasted_iota(jnp.int32, (MLA_TQ, LANES), 1)
    scores = []
    for hh in range(2):
        sl = slice(hh * LANES, (hh + 1) * LANES)
        scores.append(lax.dot_general(q_ref[:, sl], k_ref[:, sl], (((1,), (1,)), ((), ())),
                                      preferred_element_type=F32))
    outs = []
    for s in scores:
        p = jnp.exp2(s - jnp.max(s, axis=-1, keepdims=True))
        den = jnp.sum(p, axis=-1, keepdims=True)
        pv = jnp.dot(p.astype(BF16), v_ref[...], preferred_element_type=F32)
        outs.append(pv / den)
    o_ref[...] = jnp.where(lane < B_VDIM, outs[0], outs[1]).astype(BF16)


def _mla(qf, kf, vb, batch, seq):
    pairs = B_HEADS // 2
    return pl.pallas_call(
        _mla_kernel,
        grid=(batch, pairs, seq // MLA_TQ),
        in_specs=[pl.BlockSpec((None, MLA_TQ, 2 * LANES), lambda b, hp, i: (b, i, hp)),
                  pl.BlockSpec((None, seq, 2 * LANES), lambda b, hp, i: (b, 0, hp)),
                  pl.BlockSpec((None, seq, LANES), lambda b, hp, i: (b, 0, hp))],
        out_specs=pl.BlockSpec((None, MLA_TQ, LANES), lambda b, hp, i: (b, i, hp)),
        out_shape=jax.ShapeDtypeStruct((batch, seq, B_WIDTH), BF16),
        compiler_params=_params("parallel", "parallel", "parallel"),
        name="mla",
    )(qf.reshape(batch, seq, -1), kf.reshape(batch, seq, -1), vb.reshape(batch, seq, -1))


def _post_mix_kernel(x_ref, oa_ref, ob_ref, ga_ref, gb_ref, woa_ref, wob_ref, gffn_ref,
                     wr2_ref, wrh_ref, br_ref, tri_ref,
                     h_ref, route_ref, cnt_ref, carry_ref):
    i = pl.program_id(0)

    @pl.when(i == 0)
    def _():
        carry_ref[...] = jnp.zeros_like(carry_ref)

    na = _rms(oa_ref[...].astype(F32), ga_ref[...]).astype(BF16)
    nb = _rms(ob_ref[...].astype(F32), gb_ref[...]).astype(BF16)
    h = x_ref[...] + jnp.dot(na, woa_ref[...], preferred_element_type=F32)
    h = h + jnp.dot(nb, wob_ref[...], preferred_element_type=F32)
    h_ref[...] = h
    m = _rms(h, gffn_ref[...])
    tm = m.shape[0]

    m_hi = m.astype(BF16)
    m_lo = (m - m_hi.astype(F32)).astype(BF16)
    part = jnp.dot(m_hi, wr2_ref[...], preferred_element_type=F32)
    logits = part[:, :LANES] + part[:, LANES:] + br_ref[...]
    logits = logits + jnp.dot(m_lo, wrh_ref[...], preferred_element_type=F32)

    lane_i = lax.broadcasted_iota(jnp.int32, (tm, LANES), 1)
    lane = lane_i.astype(F32)
    big = float(LANES)
    is_group = (lane_i >= N_EXPERTS) & (lane_i < N_EXPERTS + N_GROUPS)
    lg = jnp.where(is_group, logits, -jnp.inf)
    g_max = jnp.max(lg, axis=-1, keepdims=True)
    g_idx = jnp.min(jnp.where(lg == g_max, lane - N_EXPERTS, big), axis=-1, keepdims=True)
    g_val = 1.0 / jnp.sum(jnp.exp(lg - g_max), axis=-1, keepdims=True)

    group_of_lane = (lane_i // EXPERTS_PER_GROUP).astype(F32)
    in_group = (lane_i < N_EXPERTS) & (group_of_lane == g_idx)
    le = jnp.where(in_group, logits, -jnp.inf)
    e_max = jnp.max(le, axis=-1, keepdims=True)
    ee = jnp.exp(le - e_max)
    p2 = jnp.where(in_group, ee / jnp.sum(ee, axis=-1, keepdims=True), -1.0)
    v1 = jnp.max(p2, axis=-1, keepdims=True)
    i1 = jnp.min(jnp.where(p2 == v1, lane, big), axis=-1, keepdims=True)
    p2b = jnp.where(lane == i1, -1.0, p2)
    v2 = jnp.max(p2b, axis=-1, keepdims=True)
    i2 = jnp.min(jnp.where(p2b == v2, lane, big), axis=-1, keepdims=True)
    norm = g_val / (v1 + v2)
    gate1 = v1 * norm
    gate2 = v2 * norm

    first_is_lo = i1 < i2
    e_lo = jnp.where(first_is_lo, i1, i2)
    e_hi = jnp.where(first_is_lo, i2, i1)
    g_lo = jnp.where(first_is_lo, gate1, gate2)
    g_hi = jnp.where(first_is_lo, gate2, gate1)
    key = e_lo * EXPERTS_PER_GROUP + (e_hi - g_idx * EXPERTS_PER_GROUP)

    key_lane = lax.broadcasted_iota(jnp.int32, (tm, N_PAIR_KEYS), 1).astype(F32)
    onehot = key_lane == key
    prefix = jnp.dot(tri_ref[...], onehot.astype(BF16), preferred_element_type=F32)
    carry = carry_ref[...]
    rank = jnp.sum(jnp.where(onehot, carry + prefix, 0.0), axis=-1, keepdims=True)
    carry = carry + jnp.sum(onehot.astype(F32), axis=0, keepdims=True)
    carry_ref[...] = carry
    cnt_ref[...] = jnp.broadcast_to(carry, cnt_ref.shape)

    route = jnp.where(lane_i == ROUTE_GATE_LO, g_lo, jnp.where(lane_i == ROUTE_GATE_HI, g_hi, 0.0))
    route_ref[...] = jnp.where(lane_i == ROUTE_KEY, key, jnp.where(lane_i == ROUTE_RANK, rank, route))


def _post_mix(x2, oa, ob, ga, gb, woa, wob, gffn, wr2, wrh, br, tri):
    t, d = x2.shape
    tm = PROJ_TILE
    full = lambda arr: pl.BlockSpec(arr.shape, lambda i: (0, 0))
    row = lambda w: pl.BlockSpec((tm, w), lambda i: (i, 0))
    return pl.pallas_call(
        _post_mix_kernel,
        grid=(t // tm,),
        in_specs=[row(d), row(A_WIDTH), row(B_WIDTH), full(ga), full(gb), full(woa), full(wob),
                  full(gffn), full(wr2), full(wrh), full(br), full(tri)],
        out_specs=[row(d), row(LANES), pl.BlockSpec((SUBLANES, N_PAIR_KEYS), lambda i: (0, 0))],
        out_shape=[jax.ShapeDtypeStruct((t, d), F32),
                   jax.ShapeDtypeStruct((t, LANES), F32),
                   jax.ShapeDtypeStruct((SUBLANES, N_PAIR_KEYS), F32)],
        scratch_shapes=[pltpu.VMEM((1, N_PAIR_KEYS), F32)],
        compiler_params=_params("arbitrary"),
        name="post_mix",
    )(x2, oa, ob, ga, gb, woa, wob, gffn, wr2, wrh, br, tri)


def _rows_copy(src, src_token, dst, dst_token, rows_per_token, sem):
    src_rows = pl.ds(pl.multiple_of(src_token * rows_per_token, rows_per_token), rows_per_token)
    dst_rows = pl.ds(pl.multiple_of(dst_token * rows_per_token, rows_per_token), rows_per_token)
    return pltpu.make_async_copy(src.at[src_rows, :], dst.at[dst_rows, :], sem)


def _wait_all_rows(vmem_buf, hbm_ref, n_rows, sem, *, to_hbm):
    hbm_rows = hbm_ref.at[pl.ds(0, n_rows), :]
    src, dst = (vmem_buf, hbm_rows) if to_hbm else (hbm_rows, vmem_buf)
    pltpu.make_async_copy(src, dst, sem).wait()


def _with_static_slot(slot, body):
    for s in range(2):
        @pl.when(slot == s)
        def _(s=s):
            body(s)


def _dispatch_kernel(offs_ref, key_ref, rank_ref, h_ref, route_ref, gffn_ref,
                     xs_ref, pos_ref, rec_ref, sems, *, n_steps):
    i = pl.program_id(0)
    tm = h_ref.shape[0]

    def drain(s):
        _wait_all_rows(rec_ref.at[s], xs_ref, tm * REC_ROWS, sems.at[s], to_hbm=True)

    def step(s):
        @pl.when(i >= 2)
        def _():
            drain(s)

        @pl.when(i < 2)
        def _():
            rec_ref[s] = jnp.zeros(rec_ref.shape[1:], F32)

        m = _rms(h_ref[...], gffn_ref[...])
        for c in range(SUBLANES):
            rec_ref[s, pl.ds(c, tm, stride=REC_ROWS), :] = m[:, c * LANES:(c + 1) * LANES]
        rec_ref[s, pl.ds(GATE_ROW, tm, stride=REC_ROWS), :] = route_ref[...]

        def issue(j, carry):
            pos = offs_ref[key_ref[j]] + rank_ref[j]
            pos_ref[j] = pos
            _rows_copy(rec_ref.at[s], j, xs_ref, pos, REC_ROWS, sems.at[s]).start()
            return carry

        lax.fori_loop(0, tm, issue, 0, unroll=DMA_ISSUE_UNROLL)

        @pl.when(i == n_steps - 1)
        def _():
            drain(s)
            if n_steps > 1:
                drain(1 - s)

    _with_static_slot(i % 2, step)


def _dispatch(offs, key, rank, h, route, gffn):
    t, d = h.shape
    tm = TOKEN_TILE
    n_steps = t // tm
    grid_spec = pltpu.PrefetchScalarGridSpec(
        num_scalar_prefetch=1,
        grid=(n_steps,),
        in_specs=[pl.BlockSpec((tm,), lambda i, of: (i,), memory_space=pltpu.SMEM),
                  pl.BlockSpec((tm,), lambda i, of: (i,), memory_space=pltpu.SMEM),
                  pl.BlockSpec((tm, d), lambda i, of: (i, 0)),
                  pl.BlockSpec((tm, LANES), lambda i, of: (i, 0)),
                  pl.BlockSpec(gffn.shape, lambda i, of: (0, 0))],
        out_specs=[pl.BlockSpec(memory_space=pl.ANY),
                   pl.BlockSpec((tm,), lambda i, of: (i,), memory_space=pltpu.SMEM)],
        scratch_shapes=[pltpu.VMEM((2, tm * REC_ROWS, LANES), F32), pltpu.SemaphoreType.DMA((2,))],
    )
    return pl.pallas_call(
        functools.partial(_dispatch_kernel, n_steps=n_steps),
        grid_spec=grid_spec,
        out_shape=[jax.ShapeDtypeStruct((t * REC_ROWS, LANES), F32), jax.ShapeDtypeStruct((t,), jnp.int32)],
        compiler_params=_params("arbitrary"),
        name="dispatch",
    )(offs, key, rank, h, route, gffn)


def _ffn(x, wgu_ref, wd_ref):
    ff = wd_ref.shape[1]
    gu = jnp.dot(x, wgu_ref[0], preferred_element_type=F32)
    g, u = gu[:, :ff], gu[:, ff:]
    hid = (g * (1.0 / (1.0 + jnp.exp(-g))) * u).astype(BF16)
    return jnp.dot(hid, wd_ref[0], preferred_element_type=F32)


ITEM_VALID, ITEM_FIRST, ITEM_LAST = 1, 2, 4


def _experts_kernel(tile_ref, lo_ref, hi_ref, row0_ref, row1_ref, flags_ref,
                    xs_ref, wgu_lo, wd_lo, wgu_hi, wd_hi, ys_ref, x_scr, gate_scr, y_scr):
    w = pl.program_id(0)
    rows_n = EXPERT_ROWS
    flags = flags_ref[w]

    @pl.when((flags & ITEM_VALID) != 0)
    def _():
        @pl.when((flags & ITEM_FIRST) != 0)
        def _():
            x_scr[...] = _token_cols(xs_ref, rows_n, REC_ROWS).astype(BF16)
            gate_scr[...] = xs_ref[pl.ds(GATE_ROW, rows_n, stride=REC_ROWS), :]

        x = x_scr[...]
        gates = gate_scr[...]
        y = (gates[:, ROUTE_GATE_LO:ROUTE_GATE_LO + 1] * _ffn(x, wgu_lo, wd_lo)
             + gates[:, ROUTE_GATE_HI:ROUTE_GATE_HI + 1] * _ffn(x, wgu_hi, wd_hi))
        rows = tile_ref[w] * rows_n + lax.broadcasted_iota(jnp.int32, (rows_n, 1), 0)
        mine = (rows >= row0_ref[w]) & (rows < row1_ref[w])

        pltpu.store(y_scr, y, mask=jnp.broadcast_to(mine, y.shape))

        @pl.when((flags & ITEM_LAST) != 0)
        def _():
            for c in range(SUBLANES):
                ys_ref[pl.ds(c, rows_n, stride=SUBLANES), :] = y_scr[:, c * LANES:(c + 1) * LANES]


def _experts(items, xs, wgu, wd):
    n_tok = xs.shape[0] // REC_ROWS
    d, ff = wd.shape[2], wd.shape[1]
    n_items = items[0].shape[0]
    lo = lambda w, ti, lo_e, hi_e, r0, r1, fl: (lo_e[w], 0, 0)
    hi = lambda w, ti, lo_e, hi_e, r0, r1, fl: (hi_e[w], 0, 0)
    tile = lambda w, ti, lo_e, hi_e, r0, r1, fl: (ti[w], 0)
    grid_spec = pltpu.PrefetchScalarGridSpec(
        num_scalar_prefetch=6,
        grid=(n_items,),
        in_specs=[pl.BlockSpec((EXPERT_ROWS * REC_ROWS, LANES), tile),
                  pl.BlockSpec((1, d, 2 * ff), lo), pl.BlockSpec((1, ff, d), lo),
                  pl.BlockSpec((1, d, 2 * ff), hi), pl.BlockSpec((1, ff, d), hi)],
        out_specs=pl.BlockSpec((EXPERT_ROWS * SUBLANES, LANES), tile),
        scratch_shapes=[pltpu.VMEM((EXPERT_ROWS, d), BF16), pltpu.VMEM((EXPERT_ROWS, LANES), F32),
                        pltpu.VMEM((EXPERT_ROWS, d), F32)],
    )
    return pl.pallas_call(
        _experts_kernel,
        grid_spec=grid_spec,
        out_shape=jax.ShapeDtypeStruct((n_tok * SUBLANES, LANES), F32),
        compiler_params=_params("arbitrary"),
        name="experts",
    )(*items, xs, wgu, wd, wgu, wd)


def _work_items(counts, n_rows):
    n_keys = counts.shape[0]
    n_tiles = n_rows // EXPERT_ROWS
    n_items = n_tiles + N_PAIRS - 1
    offs = jnp.concatenate([jnp.zeros((1,), jnp.int32), jnp.cumsum(counts)])
    first_tile = offs[:-1] // EXPERT_ROWS
    last_tile = (offs[1:] - 1) // EXPERT_ROWS
    per_key = jnp.where(counts > 0, last_tile - first_tile + 1, 0)
    item_end = jnp.cumsum(per_key)
    item_start = item_end - per_key
    w = jnp.arange(n_items, dtype=jnp.int32)
    valid = w < item_end[-1]
    key = jnp.sum((item_end[None, :] <= w[:, None]).astype(jnp.int32), axis=1)
    key = jnp.minimum(key, n_keys - 1)
    tile = first_tile[key] + (w - item_start[key])
    last_valid = jnp.maximum(item_end[-1] - 1, 0)
    tile = jnp.where(valid, tile, tile[last_valid]).astype(jnp.int32)
    key = jnp.where(valid, key, key[last_valid])
    changes = tile[1:] != tile[:-1]
    first = jnp.concatenate([jnp.ones((1,), bool), changes])
    last = jnp.concatenate([changes, jnp.ones((1,), bool)]) | (w == last_valid)
    flags = ITEM_VALID * valid + ITEM_FIRST * first + ITEM_LAST * last
    lo = key // EXPERTS_PER_GROUP
    hi = lo // EXPERTS_PER_GROUP * EXPERTS_PER_GROUP + key % EXPERTS_PER_GROUP
    as_i32 = lambda a: a.astype(jnp.int32)
    items = tuple(as_i32(a) for a in (tile, lo, hi, offs[key], offs[key + 1], flags))
    return items, as_i32(offs)


def _finish_kernel(pos_ref, pos_next_ref, ys_ref, h_ref, p_ref, gple_ref, wpg_ref, wpp_ref, gfin_ref,
                   out_ref, rows_ref, sems, *, n_steps):
    i = pl.program_id(0)
    tm = h_ref.shape[0]

    def gather(positions, s):
        def body(j, carry):
            _rows_copy(ys_ref, positions[j], rows_ref.at[s], j, SUBLANES, sems.at[s]).start()
            return carry
        lax.fori_loop(0, tm, body, 0, unroll=DMA_ISSUE_UNROLL)

    def step(s):
        @pl.when(i == 0)
        def _():
            gather(pos_ref, s)

        @pl.when(i + 1 < n_steps)
        def _():
            gather(pos_next_ref, 1 - s)

        _wait_all_rows(rows_ref.at[s], ys_ref, tm * SUBLANES, sems.at[s], to_hbm=False)
        h = h_ref[...] + _token_cols(rows_ref, tm, SUBLANES, lead=s)
        gate_in = _rms(h, gple_ref[...]).astype(BF16)
        z = jnp.dot(gate_in, wpg_ref[...], preferred_element_type=F32)
        gate = 1.0 / (1.0 + jnp.exp(-z))
        h = h + gate * jnp.dot(p_ref[...].astype(BF16), wpp_ref[...], preferred_element_type=F32)
        out_ref[...] = _rms(h, gfin_ref[...])

    _with_static_slot(i % 2, step)


def _finish(pos, ys, h, p2, gple, wpg, wpp, gfin):
    t, d = h.shape
    tm = TOKEN_TILE
    full = lambda arr: pl.BlockSpec(arr.shape, lambda i: (0, 0))
    row = lambda w: pl.BlockSpec((tm, w), lambda i: (i, 0))
    n_steps = t // tm
    return pl.pallas_call(
        functools.partial(_finish_kernel, n_steps=n_steps),
        grid=(n_steps,),
        in_specs=[pl.BlockSpec((tm,), lambda i: (i,), memory_space=pltpu.SMEM),
                  pl.BlockSpec((tm,), lambda i: (jnp.minimum(i + 1, n_steps - 1),), memory_space=pltpu.SMEM),
                  pl.BlockSpec(memory_space=pl.ANY), row(d), row(p2.shape[1]),
                  full(gple), full(wpg), full(wpp), full(gfin)],
        out_specs=row(d),
        out_shape=jax.ShapeDtypeStruct((t, d), F32),
        scratch_shapes=[pltpu.VMEM((2, tm * SUBLANES, LANES), F32), pltpu.SemaphoreType.DMA((2,))],
        compiler_params=_params("arbitrary"),
        name="finish",
    )(pos, pos, ys, h, p2, gple, wpg, wpp, gfin)


def _layer_weights(w_in, w_uq, w_ukv, seq):
    d = w_in.shape[0]
    dq, dkv = w_uq.shape[0], w_ukv.shape[0]
    o = 3 * A_WIDTH + dq + dkv
    pad = jnp.zeros((d, LANES - B_ROPE), F32)
    w_q = w_in[:, :A_WIDTH] * (A_HEAD_DIM ** -0.5)
    w1 = jnp.concatenate([w_q, w_in[:, A_WIDTH:o + B_ROPE], pad], axis=1).astype(BF16)

    uq = w_uq.reshape(dq, B_HEADS, B_NOPE + B_ROPE)
    zq = jnp.zeros((dq, B_HEADS, LANES - B_NOPE - B_ROPE), F32)
    wqm = jnp.concatenate([uq, zq], axis=-1).reshape(dq, B_HEADS * LANES).astype(BF16)

    ukv = w_ukv.reshape(dkv, B_HEADS, B_NOPE + B_VDIM)
    wk = jnp.concatenate([ukv[..., :B_NOPE], jnp.zeros((dkv, B_HEADS, LANES - B_NOPE), F32)],
                         axis=-1).reshape(dkv, B_HEADS * LANES).astype(BF16)
    wv = ukv[..., B_NOPE:].reshape(dkv, B_WIDTH).astype(BF16)

    inv_freq = 1.0 / (ROPE_THETA ** (jnp.arange(0, B_ROPE, 2, dtype=F32) / B_ROPE))
    ang = jnp.arange(seq, dtype=F32)[:, None] * inv_freq[None, :]
    cos, sin = jnp.cos(ang), jnp.sin(ang)
    scale = (B_NOPE + B_ROPE) ** -0.5 * math.log2(math.e)
    zr = jnp.zeros((seq, LANES - B_NOPE - B_ROPE), F32)
    cosq = scale * jnp.concatenate([jnp.ones((seq, B_NOPE), F32), cos, cos, zr], axis=1)
    sinq = scale * jnp.concatenate([jnp.zeros((seq, B_NOPE), F32), -sin, sin, zr], axis=1)
    zk = jnp.zeros((seq, LANES - B_ROPE), F32)
    cosk = jnp.concatenate([cos, cos, zk], axis=1)
    sink = jnp.concatenate([-sin, sin, zk], axis=1)
    return w1, wqm, wk, wv, cosq, sinq, cosk, sink


def _router_weights(w_r1, b_r1, w_r2, b_r2):
    d = w_r1.shape[0]
    w2 = jnp.transpose(w_r2, (1, 0, 2)).reshape(d, N_EXPERTS)
    padw = jnp.zeros((d, LANES - N_EXPERTS - N_GROUPS), F32)
    wr = jnp.concatenate([w2, w_r1, padw], axis=1)
    wr_hi = wr.astype(BF16)
    wr_lo = (wr - wr_hi.astype(F32)).astype(BF16)
    br = jnp.concatenate([b_r2.reshape(N_EXPERTS), b_r1, jnp.zeros((LANES - N_EXPERTS - N_GROUPS,), F32)])
    return jnp.concatenate([wr_hi, wr_lo], axis=1), wr_hi, br[None, :]


def _layer(h2, p2, batch, seq, g_mix, w_in, g_cq, w_uq, g_ckv, w_ukv, g_out_a, g_out_b, w_o, g_ffn,
           w_r1, b_r1, w_r2, b_r2, w_e_gate, w_e_up, w_e_down, g_ple, w_ple_gate, w_ple_proj, g_out):
    t, d = h2.shape
    assert d == SUBLANES * LANES, "a token's activations must fill exactly one (8, 128) tile"
    w1, wqm, wk, wv, cosq, sinq, cosk, sink = _layer_weights(w_in, w_uq, w_ukv, seq)
    qkv, qf, kf, vb = _in_proj(h2, g_mix[None], w1, g_cq[None], wqm, g_ckv[None], wk, wv,
                               cosq, sinq, cosk, sink, seq)

    slopes = jnp.exp2(-8.0 * (jnp.arange(A_HEADS, dtype=F32) + 1.0) / A_HEADS)
    for window, dil in DILATED_BRANCHES:
        assert window // (2 * dil) == BAND_HALF
    bias = jnp.stack([_band_bias(slopes * dil) for _, dil in DILATED_BRANCHES])
    oa = _dilated(qkv.reshape(batch, seq, -1), bias, batch, seq).reshape(t, A_WIDTH)
    ob = _mla(qf, kf, vb, batch, seq).reshape(t, B_WIDTH)

    wr2, wrh, br = _router_weights(w_r1, b_r1, w_r2, b_r2)
    tri = (jnp.arange(PROJ_TILE)[:, None] > jnp.arange(PROJ_TILE)[None, :]).astype(BF16)
    h1, route, cnt = _post_mix(
        h2, oa, ob, g_out_a[None], g_out_b[None], w_o[:A_WIDTH].astype(BF16), w_o[A_WIDTH:].astype(BF16),
        g_ffn[None], wr2, wrh, br, tri)

    counts = cnt[0].astype(jnp.int32)
    items, offs = _work_items(counts, t)
    xs, pos = _dispatch(offs, route[:, ROUTE_KEY].astype(jnp.int32), route[:, ROUTE_RANK].astype(jnp.int32),
                        h1, route, g_ffn[None])
    w_gate_up = jnp.concatenate([w_e_gate, w_e_up], axis=-1).astype(BF16)
    ys = _experts(items, xs, w_gate_up, w_e_down.astype(BF16))
    return _finish(pos, ys, h1, p2, g_ple[None], w_ple_gate.astype(BF16),
                   w_ple_proj.astype(BF16), g_out[None])


def kernel(x, p, g_mix, w_in, g_cq, w_uq, g_ckv, w_ukv, g_out_a, g_out_b, w_o, g_ffn, w_r1, b_r1, w_r2,
           b_r2, w_e_gate, w_e_up, w_e_down, g_ple, w_ple_gate, w_ple_proj, g_final):
    batch, seq, d = x.shape
    depth = p.shape[0]
    assert depth == 1, "the final norm is fused into the single layer's last kernel"
    h = x.reshape(batch * seq, d)
    i = 0
    h = _layer(h, p[i].reshape(batch * seq, -1), batch, seq, g_mix[i], w_in[i], g_cq[i], w_uq[i],
               g_ckv[i], w_ukv[i], g_out_a[i], g_out_b[i], w_o[i], g_ffn[i], w_r1[i], b_r1[i], w_r2[i],
               b_r2[i], w_e_gate[i], w_e_up[i], w_e_down[i], g_ple[i], w_ple_gate[i], w_ple_proj[i],
               g_final)
    return h.reshape(batch, seq, d)
```

```python
import functools
import math

import jax
import jax.numpy as jnp
import numpy as np
from jax import lax
from jax.experimental import pallas as pl
from jax.experimental.pallas import tpu as pltpu

F32 = jnp.float32
BF16 = jnp.bfloat16

EPS = 1e-6
MASK_VALUE = -1e30
LOG2_E = math.log2(math.e)
LANES = 128
SUBLANES = 8

A_HEADS = 8
A_HEAD_DIM = 64
A_WIDTH = A_HEADS * A_HEAD_DIM
DILATED_BRANCHES = ((128, 1), (512, 4), (2048, 16))

B_HEADS = 8
B_NOPE = 64
B_ROPE = 32
B_VDIM = 64
B_WIDTH = B_HEADS * B_VDIM
ROPE_THETA = 10000.0

N_GROUPS = 4
EXPERTS_PER_GROUP = 8
N_EXPERTS = N_GROUPS * EXPERTS_PER_GROUP
N_PAIR_KEYS = N_EXPERTS * EXPERTS_PER_GROUP
N_PAIRS = N_GROUPS * EXPERTS_PER_GROUP * (EXPERTS_PER_GROUP - 1) // 2

VMEM_LIMIT = 56 * 1024 * 1024

PROJ_TILE = 512
TOKEN_TILE = 256
BAND_TILE = 128
BAND_HALF = 64
BAND_KEYS = 256
BAND_UNROLL = 16
DEINT = 4
MERGE_ROWS = 256
MLA_TQ = 256
MLA_Q_TILES = 4
EXPERT_ROWS = 256
REC_ROWS = 2 * SUBLANES
GATE_ROW = SUBLANES
DMA_ISSUE_UNROLL = 8
ROUTE_GATE_LO, ROUTE_GATE_HI, ROUTE_KEY, ROUTE_RANK = 0, 1, 2, 3


def _rms(x, g):
    return x * lax.rsqrt(jnp.mean(x * x, axis=-1, keepdims=True) + EPS) * g


def _params(*sem):
    return pltpu.CompilerParams(dimension_semantics=sem, vmem_limit_bytes=VMEM_LIMIT)


def _token_cols(ref, n_tokens, rows_per_token, lead=None):
    def rows(c):
        sl = pl.ds(c, n_tokens, stride=rows_per_token)
        return ref[sl, :] if lead is None else ref[lead, sl, :]
    return jnp.concatenate([rows(c) for c in range(SUBLANES)], axis=1)


def _swap_rope_halves(x, first_lane):
    half = B_ROPE // 2
    lane = lax.broadcasted_iota(jnp.int32, x.shape, 1)
    from_above = pltpu.roll(x, LANES - half, 1)
    from_below = pltpu.roll(x, half, 1)
    return jnp.where(lane < first_lane + half, from_above, from_below)


def _in_proj_kernel(x_ref, gmix_ref, w1_ref, gcq_ref, wqm_ref, gckv_ref, wk_ref, wv_ref,
                    cosq_ref, sinq_ref, cosk_ref, sink_ref,
                    qkv_ref, qf_ref, kf_ref, vb_ref, *, dq, dkv):
    a = _rms(x_ref[...], gmix_ref[...]).astype(BF16)
    o = 3 * A_WIDTH
    proj = jnp.dot(a, w1_ref[:, o:], preferred_element_type=F32)
    cq = _rms(proj[:, 0:dq], gcq_ref[...]).astype(BF16)
    ckv = _rms(proj[:, dq:dq + dkv], gckv_ref[...]).astype(BF16)
    kr = proj[:, dq + dkv:dq + dkv + LANES]

    qm = jnp.dot(cq, wqm_ref[...], preferred_element_type=F32)
    cosq = cosq_ref[...]
    sinq = sinq_ref[...]
    for h in range(B_HEADS):
        sl = slice(h * LANES, (h + 1) * LANES)
        qf_ref[:, sl] = (qm[:, sl] * cosq + _swap_rope_halves(qm[:, sl], B_NOPE) * sinq).astype(BF16)

    krot = kr * cosk_ref[...] + _swap_rope_halves(kr, 0) * sink_ref[...]
    krot = pltpu.roll(krot, B_NOPE, 1)
    kf = jnp.dot(ckv, wk_ref[...], preferred_element_type=F32)
    for h in range(B_HEADS):
        sl = slice(h * LANES, (h + 1) * LANES)
        kf_ref[:, sl] = (kf[:, sl] + krot).astype(BF16)
    vb_ref[...] = jnp.dot(ckv, wv_ref[...], preferred_element_type=F32).astype(BF16)
    qkv = jnp.dot(a, w1_ref[:, :o], preferred_element_type=F32)
    qkv_ref[:, :A_WIDTH] = qkv[:, :A_WIDTH] * LOG2_E
    qkv_ref[:, A_WIDTH:] = qkv[:, A_WIDTH:]


def _in_proj(x2, gmix, w1, gcq, wqm, gckv, wk, wv, cosq, sinq, cosk, sink, seq):
    t, d = x2.shape
    tm = PROJ_TILE
    n_pos = seq // tm
    dq, dkv = wqm.shape[0], wk.shape[0]
    full = lambda arr: pl.BlockSpec(arr.shape, lambda i: (0, 0))
    tab = pl.BlockSpec((tm, LANES), lambda i: (i % n_pos, 0))
    row = lambda w: pl.BlockSpec((tm, w), lambda i: (i, 0))
    outs = [(3 * A_WIDTH, F32), (B_HEADS * LANES, BF16), (B_HEADS * LANES, BF16), (B_WIDTH, BF16)]
    return pl.pallas_call(
        functools.partial(_in_proj_kernel, dq=dq, dkv=dkv),
        grid=(t // tm,),
        in_specs=[row(d), full(gmix), full(w1), full(gcq), full(wqm), full(gckv),
                  full(wk), full(wv), tab, tab, tab, tab],
        out_specs=[row(w) for w, _ in outs],
        out_shape=[jax.ShapeDtypeStruct((t, w), dt) for w, dt in outs],
        compiler_params=_params("parallel"),
        name="in_proj",
    )(x2, gmix, w1, gcq, wqm, gckv, wk, wv, cosq, sinq, cosk, sink)


def _band_bias(slopes_scaled):
    qi = np.arange(BAND_TILE)[:, None]
    kc = np.arange(BAND_KEYS)[None, :]
    variants = []
    for shift in (0, BAND_HALF, BAND_KEYS - BAND_TILE):
        rel = np.abs(kc - shift - qi).astype(np.float64)
        bias = -slopes_scaled[:, None, None] * rel[None]
        variants.append(np.where(rel[None] <= BAND_HALF, bias, MASK_VALUE))
    return np.stack(variants, axis=1)


def _dilated_kernel(q_ref, k_ref, v_ref, bias_ref, out_ref, pv_scr, max_scr, den_scr, deint_scr, *, seq):
    lane = lax.broadcasted_iota(jnp.int32, (BAND_TILE, LANES), 1)
    low_half = lane < A_HEAD_DIM

    sub_len = seq // DEINT
    for a, src in enumerate((q_ref, k_ref, v_ref)):
        for c in range(DEINT):
            deint_scr[a, c * sub_len:(c + 1) * sub_len, :] = src[pl.ds(c, sub_len, stride=DEINT), :]

    for n, (_, dil) in enumerate(DILATED_BRANCHES):
        cls_len = seq // dil
        tiles_per_class = cls_len // BAND_TILE

        def tile(it, carry, n=n, dil=dil, cls_len=cls_len, tiles_per_class=tiles_per_class):
            r = it // tiles_per_class
            l0 = (it % tiles_per_class) * BAND_TILE
            start = jnp.clip(l0 - BAND_HALF, 0, cls_len - BAND_KEYS)
            variant = jnp.where(l0 == 0, 0, jnp.where(l0 == cls_len - BAND_TILE, 2, 1))
            if dil == 1:
                q_rows = pl.ds(pl.multiple_of(l0, BAND_TILE), BAND_TILE)
                k_rows = pl.ds(pl.multiple_of(start, BAND_HALF), BAND_KEYS)
                qp, kp, vp = q_ref[q_rows, :], k_ref[k_rows, :], v_ref[k_rows, :]
            else:
                q_rows = pl.ds(r + dil * l0, BAND_TILE, stride=dil)
                step = dil // DEINT
                base = (r % DEINT) * sub_len + r // DEINT
                if step == 1:
                    q_src = pl.ds(pl.multiple_of(base + l0, BAND_HALF), BAND_TILE)
                    k_src = pl.ds(pl.multiple_of(base + start, BAND_HALF), BAND_KEYS)
                else:
                    q_src = pl.ds(base + step * l0, BAND_TILE, stride=step)
                    k_src = pl.ds(base + step * start, BAND_KEYS, stride=step)
                qp, kp, vp = deint_scr[0, q_src, :], deint_scr[1, k_src, :], deint_scr[2, k_src, :]
            qp, kp, vp = qp.astype(BF16), kp.astype(BF16), vp.astype(BF16)
            pvs, maxes, dens = [], [], []
            for hh in range(2):
                qh = jnp.where(low_half if hh == 0 else ~low_half, qp, jnp.zeros_like(qp))
                s = lax.dot_general(qh, kp, (((1,), (1,)), ((), ())), preferred_element_type=F32)
                s = s + bias_ref[n, hh, variant]
                m = jnp.max(s, axis=-1, keepdims=True)
                p = jnp.exp2(s - m)
                dens.append(jnp.sum(p, axis=-1, keepdims=True))
                maxes.append(m)
                pvs.append(jnp.dot(p.astype(BF16), vp, preferred_element_type=F32))
            pv_scr[n, q_rows, :] = jnp.where(low_half, pvs[0], pvs[1])
            max_scr[n, q_rows, :] = jnp.where(low_half, maxes[0], maxes[1])
            den_scr[n, q_rows, :] = jnp.where(low_half, dens[0], dens[1])
            return carry

        lax.fori_loop(0, seq // BAND_TILE, tile, 0, unroll=BAND_UNROLL)

    def merge(c, carry):
        rows = pl.ds(pl.multiple_of(c * MERGE_ROWS, MERGE_ROWS), MERGE_ROWS)
        maxes = [max_scr[n, rows, :] for n in range(len(DILATED_BRANCHES))]
        top = functools.reduce(jnp.maximum, maxes)
        scales = [jnp.exp2(m - top) for m in maxes]
        num = sum(a * pv_scr[n, rows, :] for n, a in enumerate(scales))
        den = sum(a * den_scr[n, rows, :] for n, a in enumerate(scales))
        out_ref[rows, :] = (num / den).astype(BF16)
        return carry

    lax.fori_loop(0, seq // MERGE_ROWS, merge, 0)


def _dilated(qkv, bias, batch, seq):
    for _, dil in DILATED_BRANCHES:
        assert seq // dil >= BAND_KEYS and (seq // dil) % BAND_TILE == 0
        assert dil == 1 or dil % DEINT == 0
    pairs = A_HEADS // 2
    n_br = len(DILATED_BRANCHES)
    col = lambda base: pl.BlockSpec((None, seq, LANES), lambda b, hp: (b, 0, base + hp))
    return pl.pallas_call(
        functools.partial(_dilated_kernel, seq=seq),
        grid=(batch, pairs),
        in_specs=[col(0), col(pairs), col(2 * pairs),
                  pl.BlockSpec((n_br, 2, 3, BAND_TILE, BAND_KEYS), lambda b, hp: (0, hp, 0, 0, 0))],
        out_specs=col(0),
        out_shape=jax.ShapeDtypeStruct((batch, seq, A_WIDTH), BF16),
        scratch_shapes=[pltpu.VMEM((n_br, seq, LANES), F32)] * 3 + [pltpu.VMEM((3, seq, LANES), F32)],
        compiler_params=_params("parallel", "parallel"),
        name="dilated",
    )(qkv, qkv, qkv, bias)


def _mla_kernel(q_ref, k_ref, v_ref, o_ref):
    lane = lax.broadcasted_iota(jnp.int32, (MLA_TQ, LANES), 1)
    def qk(chain):
        qt, hh = divmod(chain, 2)
        rows = slice(qt * MLA_TQ, (qt + 1) * MLA_TQ)
        sl = slice(hh * LANES, (hh + 1) * LANES)
        return lax.dot_general(q_ref[rows, sl], k_ref[:, sl], (((1,), (1,)), ((), ())),
                               preferred_element_type=F32)

    def softmax_pv(s):
        p = jnp.exp2(s - jnp.max(s, axis=-1, keepdims=True))
        den = jnp.sum(p, axis=-1, keepdims=True)
        pv = jnp.dot(p.astype(BF16), v_ref[...], preferred_element_type=F32)
        return pv / den

    n_chains = 2 * MLA_Q_TILES
    outs = []
    s_next = qk(0)
    for chain in range(n_chains):
        s = s_next
        if chain + 1 < n_chains:
            s_next = qk(chain + 1)
        outs.append(softmax_pv(s))
    for qt in range(MLA_Q_TILES):
        rows = slice(qt * MLA_TQ, (qt + 1) * MLA_TQ)
        o_ref[rows, :] = jnp.where(lane < B_VDIM, outs[2 * qt], outs[2 * qt + 1]).astype(BF16)


def _mla(qf, kf, vb, batch, seq):
    pairs = B_HEADS // 2
    rows = MLA_TQ * MLA_Q_TILES
    return pl.pallas_call(
        _mla_kernel,
        grid=(batch, pairs, seq // rows),
        in_specs=[pl.BlockSpec((None, rows, 2 * LANES), lambda b, hp, i: (b, i, hp)),
                  pl.BlockSpec((None, seq, 2 * LANES), lambda b, hp, i: (b, 0, hp)),
                  pl.BlockSpec((None, seq, LANES), lambda b, hp, i: (b, 0, hp))],
        out_specs=pl.BlockSpec((None, rows, LANES), lambda b, hp, i: (b, i, hp)),
        out_shape=jax.ShapeDtypeStruct((batch, seq, B_WIDTH), BF16),
        compiler_params=_params("parallel", "parallel", "parallel"),
        name="mla",
    )(qf.reshape(batch, seq, -1), kf.reshape(batch, seq, -1), vb.reshape(batch, seq, -1))


def _post_mix_kernel(x_ref, oa_ref, ob_ref, ga_ref, gb_ref, woa_ref, wob_ref, gffn_ref,
                     wr2_ref, wrh_ref, br_ref, tri_ref,
                     h_ref, route_ref, cnt_ref, carry_ref):
    i = pl.program_id(0)

    @pl.when(i == 0)
    def _():
        carry_ref[...] = jnp.zeros_like(carry_ref)

    na = _rms(oa_ref[...].astype(F32), ga_ref[...]).astype(BF16)
    nb = _rms(ob_ref[...].astype(F32), gb_ref[...]).astype(BF16)
    h = x_ref[...] + jnp.dot(na, woa_ref[...], preferred_element_type=F32)
    h = h + jnp.dot(nb, wob_ref[...], preferred_element_type=F32)
    h_ref[...] = h
    m = _rms(h, gffn_ref[...])
    tm = m.shape[0]

    m_hi = m.astype(BF16)
    m_lo = (m - m_hi.astype(F32)).astype(BF16)
    part = jnp.dot(m_hi, wr2_ref[...], preferred_element_type=F32)
    logits = part[:, :LANES] + part[:, LANES:] + br_ref[...]
    logits = logits + jnp.dot(m_lo, wrh_ref[...], preferred_element_type=F32)

    lane_i = lax.broadcasted_iota(jnp.int32, (tm, LANES), 1)
    lane = lane_i.astype(F32)
    big = float(LANES)
    is_group = (lane_i >= N_EXPERTS) & (lane_i < N_EXPERTS + N_GROUPS)
    lg = jnp.where(is_group, logits, -jnp.inf)
    g_max = jnp.max(lg, axis=-1, keepdims=True)
    g_idx = jnp.min(jnp.where(lg == g_max, lane - N_EXPERTS, big), axis=-1, keepdims=True)
    g_val = 1.0 / jnp.sum(jnp.exp(lg - g_max), axis=-1, keepdims=True)

    group_of_lane = (lane_i // EXPERTS_PER_GROUP).astype(F32)
    in_group = (lane_i < N_EXPERTS) & (group_of_lane == g_idx)
    le = jnp.where(in_group, logits, -jnp.inf)
    e_max = jnp.max(le, axis=-1, keepdims=True)
    ee = jnp.exp(le - e_max)
    p2 = jnp.where(in_group, ee / jnp.sum(ee, axis=-1, keepdims=True), -1.0)
    v1 = jnp.max(p2, axis=-1, keepdims=True)
    i1 = jnp.min(jnp.where(p2 == v1, lane, big), axis=-1, keepdims=True)
    p2b = jnp.where(lane == i1, -1.0, p2)
    v2 = jnp.max(p2b, axis=-1, keepdims=True)
    i2 = jnp.min(jnp.where(p2b == v2, lane, big), axis=-1, keepdims=True)
    norm = g_val / (v1 + v2)
    gate1 = v1 * norm
    gate2 = v2 * norm

    first_is_lo = i1 < i2
    e_lo = jnp.where(first_is_lo, i1, i2)
    e_hi = jnp.where(first_is_lo, i2, i1)
    g_lo = jnp.where(first_is_lo, gate1, gate2)
    g_hi = jnp.where(first_is_lo, gate2, gate1)
    key = e_lo * EXPERTS_PER_GROUP + (e_hi - g_idx * EXPERTS_PER_GROUP)

    key_lane = lax.broadcasted_iota(jnp.int32, (tm, N_PAIR_KEYS), 1).astype(F32)
    onehot = key_lane == key
    prefix = jnp.dot(tri_ref[...], onehot.astype(BF16), preferred_element_type=F32)
    carry = carry_ref[...]
    rank = jnp.sum(jnp.where(onehot, carry + prefix, 0.0), axis=-1, keepdims=True)
    carry = carry + jnp.sum(onehot.astype(F32), axis=0, keepdims=True)
    carry_ref[...] = carry
    cnt_ref[...] = jnp.broadcast_to(carry, cnt_ref.shape)

    route = jnp.where(lane_i == ROUTE_GATE_LO, g_lo, jnp.where(lane_i == ROUTE_GATE_HI, g_hi, 0.0))
    route_ref[...] = jnp.where(lane_i == ROUTE_KEY, key, jnp.where(lane_i == ROUTE_RANK, rank, route))


def _post_mix(x2, oa, ob, ga, gb, woa, wob, gffn, wr2, wrh, br, tri):
    t, d = x2.shape
    tm = PROJ_TILE
    full = lambda arr: pl.BlockSpec(arr.shape, lambda i: (0, 0))
    row = lambda w: pl.BlockSpec((tm, w), lambda i: (i, 0))
    return pl.pallas_call(
        _post_mix_kernel,
        grid=(t // tm,),
        in_specs=[row(d), row(A_WIDTH), row(B_WIDTH), full(ga), full(gb), full(woa), full(wob),
                  full(gffn), full(wr2), full(wrh), full(br), full(tri)],
        out_specs=[row(d), row(LANES), pl.BlockSpec((SUBLANES, N_PAIR_KEYS), lambda i: (0, 0))],
        out_shape=[jax.ShapeDtypeStruct((t, d), F32),
                   jax.ShapeDtypeStruct((t, LANES), F32),
                   jax.ShapeDtypeStruct((SUBLANES, N_PAIR_KEYS), F32)],
        scratch_shapes=[pltpu.VMEM((1, N_PAIR_KEYS), F32)],
        compiler_params=_params("arbitrary"),
        name="post_mix",
    )(x2, oa, ob, ga, gb, woa, wob, gffn, wr2, wrh, br, tri)


def _rows_copy(src, src_token, dst, dst_token, rows_per_token, sem):
    src_rows = pl.ds(pl.multiple_of(src_token * rows_per_token, rows_per_token), rows_per_token)
    dst_rows = pl.ds(pl.multiple_of(dst_token * rows_per_token, rows_per_token), rows_per_token)
    return pltpu.make_async_copy(src.at[src_rows, :], dst.at[dst_rows, :], sem)


def _wait_all_rows(vmem_buf, hbm_ref, n_rows, sem, *, to_hbm):
    hbm_rows = hbm_ref.at[pl.ds(0, n_rows), :]
    src, dst = (vmem_buf, hbm_rows) if to_hbm else (hbm_rows, vmem_buf)
    pltpu.make_async_copy(src, dst, sem).wait()


def _with_static_slot(slot, body):
    for s in range(2):
        @pl.when(slot == s)
        def _(s=s):
            body(s)


def _dispatch_kernel(offs_ref, key_ref, rank_ref, h_ref, route_ref, gffn_ref,
                     xs_ref, pos_ref, rec_a, rec_b, rec_c, sems, *, n_tiles):
    i = pl.program_id(0)
    tm = h_ref.shape[0]
    recs = (rec_a, rec_b, rec_c)

    def drain(b):
        _wait_all_rows(recs[b], xs_ref, tm * REC_ROWS, sems.at[b], to_hbm=True)

    def build(b):
        m = _rms(h_ref[...], gffn_ref[...])
        for c in range(SUBLANES):
            recs[b][pl.ds(c, tm, stride=REC_ROWS), :] = m[:, c * LANES:(c + 1) * LANES]
        recs[b][pl.ds(GATE_ROW, tm, stride=REC_ROWS), :] = route_ref[...]

    def scatter(b):
        for j in range(tm):
            pos = offs_ref[key_ref[j]] + rank_ref[j]
            pos_ref[j] = pos
            _rows_copy(recs[b], j, xs_ref, pos, REC_ROWS, sems.at[b]).start()

    for b in range(3):
        prev = (b + 2) % 3

        @pl.when(i % 3 == b)
        def _(b=b, prev=prev):
            @pl.when(i >= 3)
            def _():
                drain(b)

            @pl.when(i < 3)
            def _():
                recs[b][...] = jnp.zeros(recs[b].shape, F32)

            @pl.when((i >= 1) & (i < n_tiles))
            def _():
                build(b)
                scatter(prev)

            if b == 0:
                @pl.when(i == 0)
                def _():
                    build(b)

            if b == n_tiles % 3:
                @pl.when(i == n_tiles)
                def _():
                    scatter(prev)
                    drain(prev)
                    drain((prev + 2) % 3)


def _dispatch(offs, key, rank, h, route, gffn):
    t, d = h.shape
    tm = TOKEN_TILE
    n_tiles = t // tm
    assert n_tiles >= 3
    built = lambda i, of: (jnp.minimum(i, n_tiles - 1), 0)
    scattered = lambda i, of: (jnp.maximum(i - 1, 0),)
    grid_spec = pltpu.PrefetchScalarGridSpec(
        num_scalar_prefetch=1,
        grid=(n_tiles + 1,),
        in_specs=[pl.BlockSpec((tm,), scattered, memory_space=pltpu.SMEM),
                  pl.BlockSpec((tm,), scattered, memory_space=pltpu.SMEM),
                  pl.BlockSpec((tm, d), built),
                  pl.BlockSpec((tm, LANES), built),
                  pl.BlockSpec(gffn.shape, lambda i, of: (0, 0))],
        out_specs=[pl.BlockSpec(memory_space=pl.ANY),
                   pl.BlockSpec((tm,), scattered, memory_space=pltpu.SMEM)],
        scratch_shapes=[pltpu.VMEM((tm * REC_ROWS, LANES), F32)] * 3 + [pltpu.SemaphoreType.DMA((3,))],
    )
    return pl.pallas_call(
        functools.partial(_dispatch_kernel, n_tiles=n_tiles),
        grid_spec=grid_spec,
        out_shape=[jax.ShapeDtypeStruct((t * REC_ROWS, LANES), F32), jax.ShapeDtypeStruct((t,), jnp.int32)],
        compiler_params=_params("arbitrary"),
        name="dispatch",
    )(offs, key, rank, h, route, gffn)


def _ffn(x, wgu_ref, wd_ref):
    ff = wd_ref.shape[1]
    gu = jnp.dot(x, wgu_ref[0], preferred_element_type=F32)
    g, u = gu[:, :ff], gu[:, ff:]
    hid = (g * (1.0 / (1.0 + jnp.exp(-g))) * u).astype(BF16)
    return jnp.dot(hid, wd_ref[0], preferred_element_type=F32)


ITEM_VALID, ITEM_FIRST, ITEM_LAST = 1, 2, 4


def _experts_kernel(tile_ref, lo_ref, hi_ref, row0_ref, row1_ref, flags_ref,
                    xs_ref, wgu_lo, wd_lo, wgu_hi, wd_hi, ys_ref, x_scr, gate_scr, y_scr):
    w = pl.program_id(0)
    rows_n = EXPERT_ROWS
    flags = flags_ref[w]

    @pl.when((flags & ITEM_VALID) != 0)
    def _():
        @pl.when((flags & ITEM_FIRST) != 0)
        def _():
            x_scr[...] = _token_cols(xs_ref, rows_n, REC_ROWS).astype(BF16)
            gate_scr[...] = xs_ref[pl.ds(GATE_ROW, rows_n, stride=REC_ROWS), :]

        x = x_scr[...]
        gates = gate_scr[...]
        y = (gates[:, ROUTE_GATE_LO:ROUTE_GATE_LO + 1] * _ffn(x, wgu_lo, wd_lo)
             + gates[:, ROUTE_GATE_HI:ROUTE_GATE_HI + 1] * _ffn(x, wgu_hi, wd_hi))
        rows = tile_ref[w] * rows_n + lax.broadcasted_iota(jnp.int32, (rows_n, 1), 0)
        mine = (rows >= row0_ref[w]) & (rows < row1_ref[w])

        pltpu.store(y_scr, y, mask=jnp.broadcast_to(mine, y.shape))

        @pl.when((flags & ITEM_LAST) != 0)
        def _():
            for c in range(SUBLANES):
                ys_ref[pl.ds(c, rows_n, stride=SUBLANES), :] = y_scr[:, c * LANES:(c + 1) * LANES]


def _experts(items, xs, wgu, wd):
    n_tok = xs.shape[0] // REC_ROWS
    d, ff = wd.shape[2], wd.shape[1]
    n_items = items[0].shape[0]
    lo = lambda w, ti, lo_e, hi_e, r0, r1, fl: (lo_e[w], 0, 0)
    hi = lambda w, ti, lo_e, hi_e, r0, r1, fl: (hi_e[w], 0, 0)
    tile = lambda w, ti, lo_e, hi_e, r0, r1, fl: (ti[w], 0)
    grid_spec = pltpu.PrefetchScalarGridSpec(
        num_scalar_prefetch=6,
        grid=(n_items,),
        in_specs=[pl.BlockSpec((EXPERT_ROWS * REC_ROWS, LANES), tile),
                  pl.BlockSpec((1, d, 2 * ff), lo), pl.BlockSpec((1, ff, d), lo),
                  pl.BlockSpec((1, d, 2 * ff), hi), pl.BlockSpec((1, ff, d), hi)],
        out_specs=pl.BlockSpec((EXPERT_ROWS * SUBLANES, LANES), tile),
        scratch_shapes=[pltpu.VMEM((EXPERT_ROWS, d), BF16), pltpu.VMEM((EXPERT_ROWS, LANES), F32),
                        pltpu.VMEM((EXPERT_ROWS, d), F32)],
    )
    return pl.pallas_call(
        _experts_kernel,
        grid_spec=grid_spec,
        out_shape=jax.ShapeDtypeStruct((n_tok * SUBLANES, LANES), F32),
        compiler_params=_params("arbitrary"),
        name="experts",
    )(*items, xs, wgu, wd, wgu, wd)


def _work_items(counts, n_rows):
    n_keys = counts.shape[0]
    n_tiles = n_rows // EXPERT_ROWS
    n_items = n_tiles + N_PAIRS - 1
    offs = jnp.concatenate([jnp.zeros((1,), jnp.int32), jnp.cumsum(counts)])
    first_tile = offs[:-1] // EXPERT_ROWS
    last_tile = (offs[1:] - 1) // EXPERT_ROWS
    per_key = jnp.where(counts > 0, last_tile - first_tile + 1, 0)
    item_end = jnp.cumsum(per_key)
    item_start = item_end - per_key
    w = jnp.arange(n_items, dtype=jnp.int32)
    valid = w < item_end[-1]
    key = jnp.sum((item_end[None, :] <= w[:, None]).astype(jnp.int32), axis=1)
    key = jnp.minimum(key, n_keys - 1)
    tile = first_tile[key] + (w - item_start[key])
    last_valid = jnp.maximum(item_end[-1] - 1, 0)
    tile = jnp.where(valid, tile, tile[last_valid]).astype(jnp.int32)
    key = jnp.where(valid, key, key[last_valid])
    changes = tile[1:] != tile[:-1]
    first = jnp.concatenate([jnp.ones((1,), bool), changes])
    last = jnp.concatenate([changes, jnp.ones((1,), bool)]) | (w == last_valid)
    flags = ITEM_VALID * valid + ITEM_FIRST * first + ITEM_LAST * last
    lo = key // EXPERTS_PER_GROUP
    hi = lo // EXPERTS_PER_GROUP * EXPERTS_PER_GROUP + key % EXPERTS_PER_GROUP
    as_i32 = lambda a: a.astype(jnp.int32)
    items = tuple(as_i32(a) for a in (tile, lo, hi, offs[key], offs[key + 1], flags))
    return items, as_i32(offs)


def _finish_kernel(pos_ref, pos_next_ref, ys_ref, h_ref, p_ref, gple_ref, wpg_ref, wpp_ref, gfin_ref,
                   out_ref, rows_a, rows_b, sems, *, n_steps):
    i = pl.program_id(0)
    tm = h_ref.shape[0]
    rows = (rows_a, rows_b)

    def wait(s):
        _wait_all_rows(rows[s], ys_ref, tm * SUBLANES, sems.at[s], to_hbm=False)

    def step(s):
        @pl.when(i == 0)
        def _():
            def body(j, carry):
                _rows_copy(ys_ref, pos_ref[j], rows[s], j, SUBLANES, sems.at[s]).start()
                return carry
            lax.fori_loop(0, tm, body, 0, unroll=DMA_ISSUE_UNROLL)

        wait(s)
        h = h_ref[...] + _token_cols(rows[s], tm, SUBLANES)
        for j in range(tm):
            _rows_copy(ys_ref, pos_next_ref[j], rows[1 - s], j, SUBLANES, sems.at[1 - s]).start()
        gate_in = _rms(h, gple_ref[...]).astype(BF16)
        z = jnp.dot(gate_in, wpg_ref[...], preferred_element_type=F32)
        gate = 1.0 / (1.0 + jnp.exp(-z))
        h = h + gate * jnp.dot(p_ref[...].astype(BF16), wpp_ref[...], preferred_element_type=F32)
        out_ref[...] = _rms(h, gfin_ref[...])

        @pl.when(i == n_steps - 1)
        def _():
            wait(1 - s)

    _with_static_slot(i % 2, step)


def _finish(pos, ys, h, p2, gple, wpg, wpp, gfin):
    t, d = h.shape
    tm = TOKEN_TILE
    full = lambda arr: pl.BlockSpec(arr.shape, lambda i: (0, 0))
    row = lambda w: pl.BlockSpec((tm, w), lambda i: (i, 0))
    n_steps = t // tm
    return pl.pallas_call(
        functools.partial(_finish_kernel, n_steps=n_steps),
        grid=(n_steps,),
        in_specs=[pl.BlockSpec((tm,), lambda i: (i,), memory_space=pltpu.SMEM),
                  pl.BlockSpec((tm,), lambda i: (jnp.minimum(i + 1, n_steps - 1),), memory_space=pltpu.SMEM),
                  pl.BlockSpec(memory_space=pl.ANY), row(d), row(p2.shape[1]),
                  full(gple), full(wpg), full(wpp), full(gfin)],
        out_specs=row(d),
        out_shape=jax.ShapeDtypeStruct((t, d), F32),
        scratch_shapes=[pltpu.VMEM((tm * SUBLANES, LANES), F32), pltpu.VMEM((tm * SUBLANES, LANES), F32),
                        pltpu.SemaphoreType.DMA((2,))],
        compiler_params=_params("arbitrary"),
        name="finish",
    )(pos, pos, ys, h, p2, gple, wpg, wpp, gfin)


def _layer_weights(w_in, w_uq, w_ukv, seq):
    d = w_in.shape[0]
    dq, dkv = w_uq.shape[0], w_ukv.shape[0]
    o = 3 * A_WIDTH + dq + dkv
    pad = jnp.zeros((d, LANES - B_ROPE), F32)
    w_q = w_in[:, :A_WIDTH] * (A_HEAD_DIM ** -0.5)
    w1 = jnp.concatenate([w_q, w_in[:, A_WIDTH:o + B_ROPE], pad], axis=1).astype(BF16)

    uq = w_uq.reshape(dq, B_HEADS, B_NOPE + B_ROPE)
    zq = jnp.zeros((dq, B_HEADS, LANES - B_NOPE - B_ROPE), F32)
    wqm = jnp.concatenate([uq, zq], axis=-1).reshape(dq, B_HEADS * LANES).astype(BF16)

    ukv = w_ukv.reshape(dkv, B_HEADS, B_NOPE + B_VDIM)
    wk = jnp.concatenate([ukv[..., :B_NOPE], jnp.zeros((dkv, B_HEADS, LANES - B_NOPE), F32)],
                         axis=-1).reshape(dkv, B_HEADS * LANES).astype(BF16)
    wv = ukv[..., B_NOPE:].reshape(dkv, B_WIDTH).astype(BF16)

    inv_freq = 1.0 / (ROPE_THETA ** (np.arange(0, B_ROPE, 2, dtype=np.float64) / B_ROPE))
    ang = np.arange(seq, dtype=np.float64)[:, None] * inv_freq[None, :]
    cos, sin = np.cos(ang), np.sin(ang)
    scale = (B_NOPE + B_ROPE) ** -0.5 * LOG2_E
    zr = np.zeros((seq, LANES - B_NOPE - B_ROPE))
    cosq = scale * np.concatenate([np.ones((seq, B_NOPE)), cos, cos, zr], axis=1)
    sinq = scale * np.concatenate([np.zeros((seq, B_NOPE)), -sin, sin, zr], axis=1)
    zk = np.zeros((seq, LANES - B_ROPE))
    cosk = np.concatenate([cos, cos, zk], axis=1)
    sink = np.concatenate([-sin, sin, zk], axis=1)
    f32 = lambda a: a.astype(np.float32)
    return w1, wqm, wk, wv, f32(cosq), f32(sinq), f32(cosk), f32(sink)


def _router_weights(w_r1, b_r1, w_r2, b_r2):
    d = w_r1.shape[0]
    w2 = jnp.transpose(w_r2, (1, 0, 2)).reshape(d, N_EXPERTS)
    padw = jnp.zeros((d, LANES - N_EXPERTS - N_GROUPS), F32)
    wr = jnp.concatenate([w2, w_r1, padw], axis=1)
    wr_hi = wr.astype(BF16)
    wr_lo = (wr - wr_hi.astype(F32)).astype(BF16)
    br = jnp.concatenate([b_r2.reshape(N_EXPERTS), b_r1, jnp.zeros((LANES - N_EXPERTS - N_GROUPS,), F32)])
    return jnp.concatenate([wr_hi, wr_lo], axis=1), wr_hi, br[None, :]


def _layer(h2, p2, batch, seq, g_mix, w_in, g_cq, w_uq, g_ckv, w_ukv, g_out_a, g_out_b, w_o, g_ffn,
           w_r1, b_r1, w_r2, b_r2, w_e_gate, w_e_up, w_e_down, g_ple, w_ple_gate, w_ple_proj, g_out):
    t, d = h2.shape
    assert d == SUBLANES * LANES, "a token's activations must fill exactly one (8, 128) tile"
    w1, wqm, wk, wv, cosq, sinq, cosk, sink = _layer_weights(w_in, w_uq, w_ukv, seq)
    qkv, qf, kf, vb = _in_proj(h2, g_mix[None], w1, g_cq[None], wqm, g_ckv[None], wk, wv,
                               cosq, sinq, cosk, sink, seq)

    slopes = np.exp2(-8.0 * (np.arange(A_HEADS, dtype=np.float64) + 1.0) / A_HEADS)
    for window, dil in DILATED_BRANCHES:
        assert window // (2 * dil) == BAND_HALF
    bias = np.stack([_band_bias(slopes * dil * LOG2_E) for _, dil in DILATED_BRANCHES]).astype(np.float32)
    oa = _dilated(qkv.reshape(batch, seq, -1), bias, batch, seq).reshape(t, A_WIDTH)
    ob = _mla(qf, kf, vb, batch, seq).reshape(t, B_WIDTH)

    wr2, wrh, br = _router_weights(w_r1, b_r1, w_r2, b_r2)
    tri = (jnp.arange(PROJ_TILE)[:, None] > jnp.arange(PROJ_TILE)[None, :]).astype(BF16)
    h1, route, cnt = _post_mix(
        h2, oa, ob, g_out_a[None], g_out_b[None], w_o[:A_WIDTH].astype(BF16), w_o[A_WIDTH:].astype(BF16),
        g_ffn[None], wr2, wrh, br, tri)

    counts = cnt[0].astype(jnp.int32)
    items, offs = _work_items(counts, t)
    xs, pos = _dispatch(offs, route[:, ROUTE_KEY].astype(jnp.int32), route[:, ROUTE_RANK].astype(jnp.int32),
                        h1, route, g_ffn[None])
    w_gate_up = jnp.concatenate([w_e_gate, w_e_up], axis=-1).astype(BF16)
    ys = _experts(items, xs, w_gate_up, w_e_down.astype(BF16))
    return _finish(pos, ys, h1, p2, g_ple[None], w_ple_gate.astype(BF16),
                   w_ple_proj.astype(BF16), g_out[None])


def kernel(x, p, g_mix, w_in, g_cq, w_uq, g_ckv, w_ukv, g_out_a, g_out_b, w_o, g_ffn, w_r1, b_r1, w_r2,
           b_r2, w_e_gate, w_e_up, w_e_down, g_ple, w_ple_gate, w_ple_proj, g_final):
    batch, seq, d = x.shape
    depth = p.shape[0]
    assert depth == 1, "the final norm is fused into the single layer's last kernel"
    h = x.reshape(batch * seq, d)
    i = 0
    h = _layer(h, p[i].reshape(batch * seq, -1), batch, seq, g_mix[i], w_in[i], g_cq[i], w_uq[i],
               g_ckv[i], w_ukv[i], g_out_a[i], g_out_b[i], w_o[i], g_ffn[i], w_r1[i], b_r1[i], w_r2[i],
               b_r2[i], w_e_gate[i], w_e_up[i], w_e_down[i], g_ple[i], w_ple_gate[i], w_ple_proj[i],
               g_final)
    return h.reshape(batch, seq, d)
```

```python
import functools
import math

import jax
import jax.numpy as jnp
import numpy as np
from jax import lax
from jax.experimental import pallas as pl
from jax.experimental.pallas import tpu as pltpu

F32 = jnp.float32
BF16 = jnp.bfloat16

EPS = 1e-6
MASK_VALUE = -1e30
LOG2_E = math.log2(math.e)
LANES = 128
SUBLANES = 8

A_HEADS = 8
A_HEAD_DIM = 64
A_WIDTH = A_HEADS * A_HEAD_DIM
DILATED_BRANCHES = ((128, 1), (512, 4), (2048, 16))

B_HEADS = 8
B_NOPE = 64
B_ROPE = 32
B_VDIM = 64
B_WIDTH = B_HEADS * B_VDIM
ROPE_THETA = 10000.0

N_GROUPS = 4
EXPERTS_PER_GROUP = 8
N_EXPERTS = N_GROUPS * EXPERTS_PER_GROUP
N_PAIR_KEYS = N_EXPERTS * EXPERTS_PER_GROUP
N_PAIRS = N_GROUPS * EXPERTS_PER_GROUP * (EXPERTS_PER_GROUP - 1) // 2

VMEM_LIMIT = 56 * 1024 * 1024

PROJ_TILE = 512
TOKEN_TILE = 256
BAND_TILE = 128
BAND_HALF = 64
BAND_KEYS = 256
BAND_UNROLL = 16
DEINT = 4
MERGE_ROWS = 256
MLA_TQ = 256
MLA_Q_TILES = 4
EXPERT_ROWS = 256
REC_ROWS = 2 * SUBLANES
GATE_ROW = SUBLANES
DMA_ISSUE_UNROLL = 8
ROUTE_GATE_LO, ROUTE_GATE_HI, ROUTE_KEY, ROUTE_RANK = 0, 1, 2, 3


def _rms(x, g):
    return x * lax.rsqrt(jnp.mean(x * x, axis=-1, keepdims=True) + EPS) * g


def _params(*sem):
    return pltpu.CompilerParams(dimension_semantics=sem, vmem_limit_bytes=VMEM_LIMIT)


def _token_cols(ref, n_tokens, rows_per_token, lead=None):
    def rows(c):
        sl = pl.ds(c, n_tokens, stride=rows_per_token)
        return ref[sl, :] if lead is None else ref[lead, sl, :]
    return jnp.concatenate([rows(c) for c in range(SUBLANES)], axis=1)


def _swap_rope_halves(x, first_lane):
    half = B_ROPE // 2
    lane = lax.broadcasted_iota(jnp.int32, x.shape, 1)
    from_above = pltpu.roll(x, LANES - half, 1)
    from_below = pltpu.roll(x, half, 1)
    return jnp.where(lane < first_lane + half, from_above, from_below)


def _in_proj_kernel(x_ref, gmix_ref, w1_ref, gcq_ref, wqm_ref, gckv_ref, wk_ref, wv_ref,
                    cosq_ref, sinq_ref, cosk_ref, sink_ref,
                    qkv_ref, qf_ref, kf_ref, vb_ref, *, dq, dkv):
    a = _rms(x_ref[...], gmix_ref[...]).astype(BF16)
    o = 3 * A_WIDTH
    proj = jnp.dot(a, w1_ref[:, o:], preferred_element_type=F32)
    cq = _rms(proj[:, 0:dq], gcq_ref[...]).astype(BF16)
    ckv = _rms(proj[:, dq:dq + dkv], gckv_ref[...]).astype(BF16)
    kr = proj[:, dq + dkv:dq + dkv + LANES]

    qm = jnp.dot(cq, wqm_ref[...], preferred_element_type=F32)
    cosq = cosq_ref[...]
    sinq = sinq_ref[...]
    for h in range(B_HEADS):
        sl = slice(h * LANES, (h + 1) * LANES)
        qf_ref[:, sl] = (qm[:, sl] * cosq + _swap_rope_halves(qm[:, sl], B_NOPE) * sinq).astype(BF16)

    krot = kr * cosk_ref[...] + _swap_rope_halves(kr, 0) * sink_ref[...]
    krot = pltpu.roll(krot, B_NOPE, 1)
    kf = jnp.dot(ckv, wk_ref[...], preferred_element_type=F32)
    for h in range(B_HEADS):
        sl = slice(h * LANES, (h + 1) * LANES)
        kf_ref[:, sl] = (kf[:, sl] + krot).astype(BF16)
    vb_ref[...] = jnp.dot(ckv, wv_ref[...], preferred_element_type=F32).astype(BF16)
    qkv = jnp.dot(a, w1_ref[:, :o], preferred_element_type=F32)
    qkv_ref[:, :A_WIDTH] = qkv[:, :A_WIDTH] * LOG2_E
    qkv_ref[:, A_WIDTH:] = qkv[:, A_WIDTH:]


def _in_proj(x2, gmix, w1, gcq, wqm, gckv, wk, wv, cosq, sinq, cosk, sink, seq):
    t, d = x2.shape
    tm = PROJ_TILE
    n_pos = seq // tm
    dq, dkv = wqm.shape[0], wk.shape[0]
    full = lambda arr: pl.BlockSpec(arr.shape, lambda i: (0, 0))
    tab = pl.BlockSpec((tm, LANES), lambda i: (i % n_pos, 0))
    row = lambda w: pl.BlockSpec((tm, w), lambda i: (i, 0))
    outs = [(3 * A_WIDTH, F32), (B_HEADS * LANES, BF16), (B_HEADS * LANES, BF16), (B_WIDTH, BF16)]
    return pl.pallas_call(
        functools.partial(_in_proj_kernel, dq=dq, dkv=dkv),
        grid=(t // tm,),
        in_specs=[row(d), full(gmix), full(w1), full(gcq), full(wqm), full(gckv),
                  full(wk), full(wv), tab, tab, tab, tab],
        out_specs=[row(w) for w, _ in outs],
        out_shape=[jax.ShapeDtypeStruct((t, w), dt) for w, dt in outs],
        compiler_params=_params("parallel"),
        name="in_proj",
    )(x2, gmix, w1, gcq, wqm, gckv, wk, wv, cosq, sinq, cosk, sink)


def _band_bias(slopes_scaled):
    qi = np.arange(BAND_TILE)[:, None]
    kc = np.arange(BAND_KEYS)[None, :]
    variants = []
    for shift in (0, BAND_HALF, BAND_KEYS - BAND_TILE):
        rel = np.abs(kc - shift - qi).astype(np.float64)
        bias = -slopes_scaled[:, None, None] * rel[None]
        variants.append(np.where(rel[None] <= BAND_HALF, bias, MASK_VALUE))
    return np.stack(variants, axis=1)


def _dilated_kernel(q_ref, k_ref, v_ref, bias_ref, out_ref, pv_scr, max_scr, den_scr, deint_scr, *, seq):
    lane = lax.broadcasted_iota(jnp.int32, (BAND_TILE, LANES), 1)
    low_half = lane < A_HEAD_DIM

    sub_len = seq // DEINT
    for a, src in enumerate((q_ref, k_ref, v_ref)):
        for c in range(DEINT):
            deint_scr[a, c * sub_len:(c + 1) * sub_len, :] = src[pl.ds(c, sub_len, stride=DEINT), :]

    for n, (_, dil) in enumerate(DILATED_BRANCHES):
        cls_len = seq // dil
        tiles_per_class = cls_len // BAND_TILE

        def tile(it, carry, n=n, dil=dil, cls_len=cls_len, tiles_per_class=tiles_per_class):
            r = it // tiles_per_class
            l0 = (it % tiles_per_class) * BAND_TILE
            start = jnp.clip(l0 - BAND_HALF, 0, cls_len - BAND_KEYS)
            variant = jnp.where(l0 == 0, 0, jnp.where(l0 == cls_len - BAND_TILE, 2, 1))
            if dil == 1:
                q_rows = pl.ds(pl.multiple_of(l0, BAND_TILE), BAND_TILE)
                k_rows = pl.ds(pl.multiple_of(start, BAND_HALF), BAND_KEYS)
                qp, kp, vp = q_ref[q_rows, :], k_ref[k_rows, :], v_ref[k_rows, :]
            else:
                q_rows = pl.ds(r + dil * l0, BAND_TILE, stride=dil)
                step = dil // DEINT
                base = (r % DEINT) * sub_len + r // DEINT
                if step == 1:
                    q_src = pl.ds(pl.multiple_of(base + l0, BAND_HALF), BAND_TILE)
                    k_src = pl.ds(pl.multiple_of(base + start, BAND_HALF), BAND_KEYS)
                else:
                    q_src = pl.ds(base + step * l0, BAND_TILE, stride=step)
                    k_src = pl.ds(base + step * start, BAND_KEYS, stride=step)
                qp, kp, vp = deint_scr[0, q_src, :], deint_scr[1, k_src, :], deint_scr[2, k_src, :]
            qp, kp, vp = qp.astype(BF16), kp.astype(BF16), vp.astype(BF16)
            pvs, maxes, dens = [], [], []
            for hh in range(2):
                qh = jnp.where(low_half if hh == 0 else ~low_half, qp, jnp.zeros_like(qp))
                s = lax.dot_general(qh, kp, (((1,), (1,)), ((), ())), preferred_element_type=F32)
                s = s + bias_ref[n, hh, variant]
                m = jnp.max(s, axis=-1, keepdims=True)
                p = jnp.exp2(s - m)
                dens.append(jnp.sum(p, axis=-1, keepdims=True))
                maxes.append(m)
                pvs.append(jnp.dot(p.astype(BF16), vp, preferred_element_type=F32))
            pv_scr[n, q_rows, :] = jnp.where(low_half, pvs[0], pvs[1])
            max_scr[n, q_rows, :] = jnp.where(low_half, maxes[0], maxes[1])
            den_scr[n, q_rows, :] = jnp.where(low_half, dens[0], dens[1])
            return carry

        lax.fori_loop(0, seq // BAND_TILE, tile, 0, unroll=BAND_UNROLL)

    def merge(c, carry):
        rows = pl.ds(pl.multiple_of(c * MERGE_ROWS, MERGE_ROWS), MERGE_ROWS)
        maxes = [max_scr[n, rows, :] for n in range(len(DILATED_BRANCHES))]
        top = functools.reduce(jnp.maximum, maxes)
        scales = [jnp.exp2(m - top) for m in maxes]
        num = sum(a * pv_scr[n, rows, :] for n, a in enumerate(scales))
        den = sum(a * den_scr[n, rows, :] for n, a in enumerate(scales))
        out_ref[rows, :] = (num / den).astype(BF16)
        return carry

    lax.fori_loop(0, seq // MERGE_ROWS, merge, 0)


def _dilated(qkv, bias, batch, seq):
    for _, dil in DILATED_BRANCHES:
        assert seq // dil >= BAND_KEYS and (seq // dil) % BAND_TILE == 0
        assert dil == 1 or dil % DEINT == 0
    pairs = A_HEADS // 2
    n_br = len(DILATED_BRANCHES)
    col = lambda base: pl.BlockSpec((None, seq, LANES), lambda b, hp: (b, 0, base + hp))
    return pl.pallas_call(
        functools.partial(_dilated_kernel, seq=seq),
        grid=(batch, pairs),
        in_specs=[col(0), col(pairs), col(2 * pairs),
                  pl.BlockSpec((n_br, 2, 3, BAND_TILE, BAND_KEYS), lambda b, hp: (0, hp, 0, 0, 0))],
        out_specs=col(0),
        out_shape=jax.ShapeDtypeStruct((batch, seq, A_WIDTH), BF16),
        scratch_shapes=[pltpu.VMEM((n_br, seq, LANES), F32)] * 3 + [pltpu.VMEM((3, seq, LANES), F32)],
        compiler_params=_params("parallel", "parallel"),
        name="dilated",
    )(qkv, qkv, qkv, bias)


def _mla_kernel(q_ref, k_ref, v_ref, o_ref):
    lane = lax.broadcasted_iota(jnp.int32, (MLA_TQ, LANES), 1)
    def qk(chain):
        qt, hh = divmod(chain, 2)
        rows = slice(qt * MLA_TQ, (qt + 1) * MLA_TQ)
        sl = slice(hh * LANES, (hh + 1) * LANES)
        return lax.dot_general(q_ref[rows, sl], k_ref[:, sl], (((1,), (1,)), ((), ())),
                               preferred_element_type=F32)

    def softmax_pv(s):
        p = jnp.exp2(s - jnp.max(s, axis=-1, keepdims=True))
        den = jnp.sum(p, axis=-1, keepdims=True)
        pv = jnp.dot(p.astype(BF16), v_ref[...], preferred_element_type=F32)
        return pv / den

    n_chains = 2 * MLA_Q_TILES
    outs = []
    s_next = qk(0)
    for chain in range(n_chains):
        s = s_next
        if chain + 1 < n_chains:
            s_next = qk(chain + 1)
        outs.append(softmax_pv(s))
    for qt in range(MLA_Q_TILES):
        rows = slice(qt * MLA_TQ, (qt + 1) * MLA_TQ)
        o_ref[rows, :] = jnp.where(lane < B_VDIM, outs[2 * qt], outs[2 * qt + 1]).astype(BF16)


def _mla(qf, kf, vb, batch, seq):
    pairs = B_HEADS // 2
    rows = MLA_TQ * MLA_Q_TILES
    return pl.pallas_call(
        _mla_kernel,
        grid=(batch, pairs, seq // rows),
        in_specs=[pl.BlockSpec((None, rows, 2 * LANES), lambda b, hp, i: (b, i, hp)),
                  pl.BlockSpec((None, seq, 2 * LANES), lambda b, hp, i: (b, 0, hp)),
                  pl.BlockSpec((None, seq, LANES), lambda b, hp, i: (b, 0, hp))],
        out_specs=pl.BlockSpec((None, rows, LANES), lambda b, hp, i: (b, i, hp)),
        out_shape=jax.ShapeDtypeStruct((batch, seq, B_WIDTH), BF16),
        compiler_params=_params("parallel", "parallel", "parallel"),
        name="mla",
    )(qf.reshape(batch, seq, -1), kf.reshape(batch, seq, -1), vb.reshape(batch, seq, -1))


def _post_mix_kernel(x_ref, oa_ref, ob_ref, ga_ref, gb_ref, woa_ref, wob_ref, gffn_ref,
                     wr2_ref, wrh_ref, br_ref, tri_ref,
                     h_ref, route_ref, cnt_ref, carry_ref):
    i = pl.program_id(0)

    @pl.when(i == 0)
    def _():
        carry_ref[...] = jnp.zeros_like(carry_ref)

    na = _rms(oa_ref[...].astype(F32), ga_ref[...]).astype(BF16)
    nb = _rms(ob_ref[...].astype(F32), gb_ref[...]).astype(BF16)
    h = x_ref[...] + jnp.dot(na, woa_ref[...], preferred_element_type=F32)
    h = h + jnp.dot(nb, wob_ref[...], preferred_element_type=F32)
    h_ref[...] = h
    m = _rms(h, gffn_ref[...])
    tm = m.shape[0]

    m_hi = m.astype(BF16)
    m_lo = (m - m_hi.astype(F32)).astype(BF16)
    part = jnp.dot(m_hi, wr2_ref[...], preferred_element_type=F32)
    logits = part[:, :LANES] + part[:, LANES:] + br_ref[...]
    logits = logits + jnp.dot(m_lo, wrh_ref[...], preferred_element_type=F32)

    lane_i = lax.broadcasted_iota(jnp.int32, (tm, LANES), 1)
    lane = lane_i.astype(F32)
    big = float(LANES)
    is_group = (lane_i >= N_EXPERTS) & (lane_i < N_EXPERTS + N_GROUPS)
    lg = jnp.where(is_group, logits, -jnp.inf)
    g_max = jnp.max(lg, axis=-1, keepdims=True)
    g_idx = jnp.min(jnp.where(lg == g_max, lane - N_EXPERTS, big), axis=-1, keepdims=True)
    g_val = 1.0 / jnp.sum(jnp.exp(lg - g_max), axis=-1, keepdims=True)

    group_of_lane = (lane_i // EXPERTS_PER_GROUP).astype(F32)
    in_group = (lane_i < N_EXPERTS) & (group_of_lane == g_idx)
    le = jnp.where(in_group, logits, -jnp.inf)
    e_max = jnp.max(le, axis=-1, keepdims=True)
    ee = jnp.exp(le - e_max)
    p2 = jnp.where(in_group, ee / jnp.sum(ee, axis=-1, keepdims=True), -1.0)
    v1 = jnp.max(p2, axis=-1, keepdims=True)
    i1 = jnp.min(jnp.where(p2 == v1, lane, big), axis=-1, keepdims=True)
    p2b = jnp.where(lane == i1, -1.0, p2)
    v2 = jnp.max(p2b, axis=-1, keepdims=True)
    i2 = jnp.min(jnp.where(p2b == v2, lane, big), axis=-1, keepdims=True)
    norm = g_val / (v1 + v2)
    gate1 = v1 * norm
    gate2 = v2 * norm

    first_is_lo = i1 < i2
    e_lo = jnp.where(first_is_lo, i1, i2)
    e_hi = jnp.where(first_is_lo, i2, i1)
    g_lo = jnp.where(first_is_lo, gate1, gate2)
    g_hi = jnp.where(first_is_lo, gate2, gate1)
    key = e_lo * EXPERTS_PER_GROUP + (e_hi - g_idx * EXPERTS_PER_GROUP)

    key_lane = lax.broadcasted_iota(jnp.int32, (tm, N_PAIR_KEYS), 1).astype(F32)
    onehot = key_lane == key
    prefix = jnp.dot(tri_ref[...], onehot.astype(BF16), preferred_element_type=F32)
    carry = carry_ref[...]
    rank = jnp.sum(jnp.where(onehot, carry + prefix, 0.0), axis=-1, keepdims=True)
    carry = carry + jnp.sum(onehot.astype(F32), axis=0, keepdims=True)
    carry_ref[...] = carry
    cnt_ref[...] = jnp.broadcast_to(carry, cnt_ref.shape)

    route = jnp.where(lane_i == ROUTE_GATE_LO, g_lo, jnp.where(lane_i == ROUTE_GATE_HI, g_hi, 0.0))
    route_ref[...] = jnp.where(lane_i == ROUTE_KEY, key, jnp.where(lane_i == ROUTE_RANK, rank, route))


def _post_mix(x2, oa, ob, ga, gb, woa, wob, gffn, wr2, wrh, br, tri):
    t, d = x2.shape
    tm = PROJ_TILE
    full = lambda arr: pl.BlockSpec(arr.shape, lambda i: (0, 0))
    row = lambda w: pl.BlockSpec((tm, w), lambda i: (i, 0))
    return pl.pallas_call(
        _post_mix_kernel,
        grid=(t // tm,),
        in_specs=[row(d), row(A_WIDTH), row(B_WIDTH), full(ga), full(gb), full(woa), full(wob),
                  full(gffn), full(wr2), full(wrh), full(br), full(tri)],
        out_specs=[row(d), row(LANES), pl.BlockSpec((SUBLANES, N_PAIR_KEYS), lambda i: (0, 0))],
        out_shape=[jax.ShapeDtypeStruct((t, d), F32),
                   jax.ShapeDtypeStruct((t, LANES), F32),
                   jax.ShapeDtypeStruct((SUBLANES, N_PAIR_KEYS), F32)],
        scratch_shapes=[pltpu.VMEM((1, N_PAIR_KEYS), F32)],
        compiler_params=_params("arbitrary"),
        name="post_mix",
    )(x2, oa, ob, ga, gb, woa, wob, gffn, wr2, wrh, br, tri)


def _rows_copy(src, src_token, dst, dst_token, rows_per_token, sem):
    src_rows = pl.ds(pl.multiple_of(src_token * rows_per_token, rows_per_token), rows_per_token)
    dst_rows = pl.ds(pl.multiple_of(dst_token * rows_per_token, rows_per_token), rows_per_token)
    return pltpu.make_async_copy(src.at[src_rows, :], dst.at[dst_rows, :], sem)


def _wait_all_rows(vmem_buf, hbm_ref, n_rows, sem, *, to_hbm):
    hbm_rows = hbm_ref.at[pl.ds(0, n_rows), :]
    src, dst = (vmem_buf, hbm_rows) if to_hbm else (hbm_rows, vmem_buf)
    pltpu.make_async_copy(src, dst, sem).wait()


def _with_static_slot(slot, body):
    for s in range(2):
        @pl.when(slot == s)
        def _(s=s):
            body(s)


def _dispatch_kernel(offs_ref, key_ref, rank_ref, h_ref, route_ref, gffn_ref,
                     xs_ref, pos_ref, rec_a, rec_b, rec_c, sems, *, n_tiles):
    i = pl.program_id(0)
    tm = h_ref.shape[0]
    recs = (rec_a, rec_b, rec_c)

    def drain(b):
        _wait_all_rows(recs[b], xs_ref, tm * REC_ROWS, sems.at[b], to_hbm=True)

    def build(b):
        m = _rms(h_ref[...], gffn_ref[...])
        for c in range(SUBLANES):
            recs[b][pl.ds(c, tm, stride=REC_ROWS), :] = m[:, c * LANES:(c + 1) * LANES]
        recs[b][pl.ds(GATE_ROW, tm, stride=REC_ROWS), :] = route_ref[...]

    def scatter(b):
        for j in range(tm):
            pos = offs_ref[key_ref[j]] + rank_ref[j]
            pos_ref[j] = pos
            _rows_copy(recs[b], j, xs_ref, pos, REC_ROWS, sems.at[b]).start()

    for b in range(3):
        prev = (b + 2) % 3

        @pl.when(i % 3 == b)
        def _(b=b, prev=prev):
            @pl.when(i >= 3)
            def _():
                drain(b)

            @pl.when(i < 3)
            def _():
                recs[b][...] = jnp.zeros(recs[b].shape, F32)

            @pl.when((i >= 1) & (i < n_tiles))
            def _():
                build(b)
                scatter(prev)

            if b == 0:
                @pl.when(i == 0)
                def _():
                    build(b)

            if b == n_tiles % 3:
                @pl.when(i == n_tiles)
                def _():
                    scatter(prev)
                    drain(prev)
                    drain((prev + 2) % 3)


def _dispatch(offs, key, rank, h, route, gffn):
    t, d = h.shape
    tm = TOKEN_TILE
    n_tiles = t // tm
    assert n_tiles >= 3
    built = lambda i, of: (jnp.minimum(i, n_tiles - 1), 0)
    scattered = lambda i, of: (jnp.maximum(i - 1, 0),)
    grid_spec = pltpu.PrefetchScalarGridSpec(
        num_scalar_prefetch=1,
        grid=(n_tiles + 1,),
        in_specs=[pl.BlockSpec((tm,), scattered, memory_space=pltpu.SMEM),
                  pl.BlockSpec((tm,), scattered, memory_space=pltpu.SMEM),
                  pl.BlockSpec((tm, d), built),
                  pl.BlockSpec((tm, LANES), built),
                  pl.BlockSpec(gffn.shape, lambda i, of: (0, 0))],
        out_specs=[pl.BlockSpec(memory_space=pl.ANY),
                   pl.BlockSpec((tm,), scattered, memory_space=pltpu.SMEM)],
        scratch_shapes=[pltpu.VMEM((tm * REC_ROWS, LANES), F32)] * 3 + [pltpu.SemaphoreType.DMA((3,))],
    )
    return pl.pallas_call(
        functools.partial(_dispatch_kernel, n_tiles=n_tiles),
        grid_spec=grid_spec,
        out_shape=[jax.ShapeDtypeStruct((t * REC_ROWS, LANES), F32), jax.ShapeDtypeStruct((t,), jnp.int32)],
        compiler_params=_params("arbitrary"),
        name="dispatch",
    )(offs, key, rank, h, route, gffn)


def _ffn(x, wgu_ref, wd_ref):
    ff = wd_ref.shape[1]
    gu = jnp.dot(x, wgu_ref[0], preferred_element_type=F32)
    g, u = gu[:, :ff], gu[:, ff:]
    hid = (g * (1.0 / (1.0 + jnp.exp(-g))) * u).astype(BF16)
    return jnp.dot(hid, wd_ref[0], preferred_element_type=F32)


ITEM_VALID, ITEM_FIRST, ITEM_LAST = 1, 2, 4


def _experts_kernel(tile_ref, lo_ref, hi_ref, row0_ref, row1_ref, flags_ref,
                    xs_ref, wgu_lo, wd_lo, wgu_hi, wd_hi, ys_ref, x_scr, gate_scr, y_scr):
    w = pl.program_id(0)
    rows_n = EXPERT_ROWS
    flags = flags_ref[w]

    @pl.when((flags & ITEM_VALID) != 0)
    def _():
        @pl.when((flags & ITEM_FIRST) != 0)
        def _():
            x_scr[...] = _token_cols(xs_ref, rows_n, REC_ROWS).astype(BF16)
            gate_scr[...] = xs_ref[pl.ds(GATE_ROW, rows_n, stride=REC_ROWS), :]

        x = x_scr[...]
        gates = gate_scr[...]
        y = (gates[:, ROUTE_GATE_LO:ROUTE_GATE_LO + 1] * _ffn(x, wgu_lo, wd_lo)
             + gates[:, ROUTE_GATE_HI:ROUTE_GATE_HI + 1] * _ffn(x, wgu_hi, wd_hi))
        rows = tile_ref[w] * rows_n + lax.broadcasted_iota(jnp.int32, (rows_n, 1), 0)
        mine = (rows >= row0_ref[w]) & (rows < row1_ref[w])

        pltpu.store(y_scr, y, mask=jnp.broadcast_to(mine, y.shape))

        @pl.when((flags & ITEM_LAST) != 0)
        def _():
            for c in range(SUBLANES):
                ys_ref[pl.ds(c, rows_n, stride=SUBLANES), :] = y_scr[:, c * LANES:(c + 1) * LANES]


def _experts(items, xs, wgu, wd):
    n_tok = xs.shape[0] // REC_ROWS
    d, ff = wd.shape[2], wd.shape[1]
    n_items = items[0].shape[0]
    lo = lambda w, ti, lo_e, hi_e, r0, r1, fl: (lo_e[w], 0, 0)
    hi = lambda w, ti, lo_e, hi_e, r0, r1, fl: (hi_e[w], 0, 0)
    tile = lambda w, ti, lo_e, hi_e, r0, r1, fl: (ti[w], 0)
    grid_spec = pltpu.PrefetchScalarGridSpec(
        num_scalar_prefetch=6,
        grid=(n_items,),
        in_specs=[pl.BlockSpec((EXPERT_ROWS * REC_ROWS, LANES), tile),
                  pl.BlockSpec((1, d, 2 * ff), lo), pl.BlockSpec((1, ff, d), lo),
                  pl.BlockSpec((1, d, 2 * ff), hi), pl.BlockSpec((1, ff, d), hi)],
        out_specs=pl.BlockSpec((EXPERT_ROWS * SUBLANES, LANES), tile),
        scratch_shapes=[pltpu.VMEM((EXPERT_ROWS, d), BF16), pltpu.VMEM((EXPERT_ROWS, LANES), F32),
                        pltpu.VMEM((EXPERT_ROWS, d), F32)],
    )
    return pl.pallas_call(
        _experts_kernel,
        grid_spec=grid_spec,
        out_shape=jax.ShapeDtypeStruct((n_tok * SUBLANES, LANES), F32),
        compiler_params=_params("arbitrary"),
        name="experts",
    )(*items, xs, wgu, wd, wgu, wd)


def _work_items(counts, n_rows):
    n_keys = counts.shape[0]
    n_tiles = n_rows // EXPERT_ROWS
    n_items = n_tiles + N_PAIRS - 1
    offs = jnp.concatenate([jnp.zeros((1,), jnp.int32), jnp.cumsum(counts)])
    first_tile = offs[:-1] // EXPERT_ROWS
    last_tile = (offs[1:] - 1) // EXPERT_ROWS
    per_key = jnp.where(counts > 0, last_tile - first_tile + 1, 0)
    item_end = jnp.cumsum(per_key)
    item_start = item_end - per_key
    w = jnp.arange(n_items, dtype=jnp.int32)
    valid = w < item_end[-1]
    key = jnp.sum((item_end[None, :] <= w[:, None]).astype(jnp.int32), axis=1)
    key = jnp.minimum(key, n_keys - 1)
    tile = first_tile[key] + (w - item_start[key])
    last_valid = jnp.maximum(item_end[-1] - 1, 0)
    tile = jnp.where(valid, tile, tile[last_valid]).astype(jnp.int32)
    key = jnp.where(valid, key, key[last_valid])
    changes = tile[1:] != tile[:-1]
    first = jnp.concatenate([jnp.ones((1,), bool), changes])
    last = jnp.concatenate([changes, jnp.ones((1,), bool)]) | (w == last_valid)
    flags = ITEM_VALID * valid + ITEM_FIRST * first + ITEM_LAST * last
    lo = key // EXPERTS_PER_GROUP
    hi = lo // EXPERTS_PER_GROUP * EXPERTS_PER_GROUP + key % EXPERTS_PER_GROUP
    as_i32 = lambda a: a.astype(jnp.int32)
    items = tuple(as_i32(a) for a in (tile, lo, hi, offs[key], offs[key + 1], flags))
    return items, as_i32(offs)


def _finish_kernel(pos_ref, pos_next_ref, ys_ref, h_ref, p_ref, gple_ref, wpg_ref, wpp_ref, gfin_ref,
                   out_ref, rows_ref, sems, *, n_steps):
    i = pl.program_id(0)
    tm = h_ref.shape[0]

    def gather(positions, s):
        def body(j, carry):
            _rows_copy(ys_ref, positions[j], rows_ref.at[s], j, SUBLANES, sems.at[s]).start()
            return carry
        lax.fori_loop(0, tm, body, 0, unroll=DMA_ISSUE_UNROLL)

    def step(s):
        @pl.when(i == 0)
        def _():
            gather(pos_ref, s)

        @pl.when(i + 1 < n_steps)
        def _():
            gather(pos_next_ref, 1 - s)

        _wait_all_rows(rows_ref.at[s], ys_ref, tm * SUBLANES, sems.at[s], to_hbm=False)
        h = h_ref[...] + _token_cols(rows_ref, tm, SUBLANES, lead=s)
        gate_in = _rms(h, gple_ref[...]).astype(BF16)
        z = jnp.dot(gate_in, wpg_ref[...], preferred_element_type=F32)
        gate = 1.0 / (1.0 + jnp.exp(-z))
        h = h + gate * jnp.dot(p_ref[...].astype(BF16), wpp_ref[...], preferred_element_type=F32)
        out_ref[...] = _rms(h, gfin_ref[...])

    _with_static_slot(i % 2, step)


def _finish(pos, ys, h, p2, gple, wpg, wpp, gfin):
    t, d = h.shape
    tm = TOKEN_TILE
    full = lambda arr: pl.BlockSpec(arr.shape, lambda i: (0, 0))
    row = lambda w: pl.BlockSpec((tm, w), lambda i: (i, 0))
    n_steps = t // tm
    return pl.pallas_call(
        functools.partial(_finish_kernel, n_steps=n_steps),
        grid=(n_steps,),
        in_specs=[pl.BlockSpec((tm,), lambda i: (i,), memory_space=pltpu.SMEM),
                  pl.BlockSpec((tm,), lambda i: (jnp.minimum(i + 1, n_steps - 1),), memory_space=pltpu.SMEM),
                  pl.BlockSpec(memory_space=pl.ANY), row(d), row(p2.shape[1]),
                  full(gple), full(wpg), full(wpp), full(gfin)],
        out_specs=row(d),
        out_shape=jax.ShapeDtypeStruct((t, d), F32),
        scratch_shapes=[pltpu.VMEM((2, tm * SUBLANES, LANES), F32), pltpu.SemaphoreType.DMA((2,))],
        compiler_params=_params("arbitrary"),
        name="finish",
    )(pos, pos, ys, h, p2, gple, wpg, wpp, gfin)


def _layer_weights(w_in, w_uq, w_ukv, seq):
    d = w_in.shape[0]
    dq, dkv = w_uq.shape[0], w_ukv.shape[0]
    o = 3 * A_WIDTH + dq + dkv
    pad = jnp.zeros((d, LANES - B_ROPE), F32)
    w_q = w_in[:, :A_WIDTH] * (A_HEAD_DIM ** -0.5)
    w1 = jnp.concatenate([w_q, w_in[:, A_WIDTH:o + B_ROPE], pad], axis=1).astype(BF16)

    uq = w_uq.reshape(dq, B_HEADS, B_NOPE + B_ROPE)
    zq = jnp.zeros((dq, B_HEADS, LANES - B_NOPE - B_ROPE), F32)
    wqm = jnp.concatenate([uq, zq], axis=-1).reshape(dq, B_HEADS * LANES).astype(BF16)

    ukv = w_ukv.reshape(dkv, B_HEADS, B_NOPE + B_VDIM)
    wk = jnp.concatenate([ukv[..., :B_NOPE], jnp.zeros((dkv, B_HEADS, LANES - B_NOPE), F32)],
                         axis=-1).reshape(dkv, B_HEADS * LANES).astype(BF16)
    wv = ukv[..., B_NOPE:].reshape(dkv, B_WIDTH).astype(BF16)

    inv_freq = 1.0 / (ROPE_THETA ** (np.arange(0, B_ROPE, 2, dtype=np.float64) / B_ROPE))
    ang = np.arange(seq, dtype=np.float64)[:, None] * inv_freq[None, :]
    cos, sin = np.cos(ang), np.sin(ang)
    scale = (B_NOPE + B_ROPE) ** -0.5 * LOG2_E
    zr = np.zeros((seq, LANES - B_NOPE - B_ROPE))
    cosq = scale * np.concatenate([np.ones((seq, B_NOPE)), cos, cos, zr], axis=1)
    sinq = scale * np.concatenate([np.zeros((seq, B_NOPE)), -sin, sin, zr], axis=1)
    zk = np.zeros((seq, LANES - B_ROPE))
    cosk = np.concatenate([cos, cos, zk], axis=1)
    sink = np.concatenate([-sin, sin, zk], axis=1)
    f32 = lambda a: a.astype(np.float32)
    return w1, wqm, wk, wv, f32(cosq), f32(sinq), f32(cosk), f32(sink)


def _router_weights(w_r1, b_r1, w_r2, b_r2):
    d = w_r1.shape[0]
    w2 = jnp.transpose(w_r2, (1, 0, 2)).reshape(d, N_EXPERTS)
    padw = jnp.zeros((d, LANES - N_EXPERTS - N_GROUPS), F32)
    wr = jnp.concatenate([w2, w_r1, padw], axis=1)
    wr_hi = wr.astype(BF16)
    wr_lo = (wr - wr_hi.astype(F32)).astype(BF16)
    br = jnp.concatenate([b_r2.reshape(N_EXPERTS), b_r1, jnp.zeros((LANES - N_EXPERTS - N_GROUPS,), F32)])
    return jnp.concatenate([wr_hi, wr_lo], axis=1), wr_hi, br[None, :]


def _layer(h2, p2, batch, seq, g_mix, w_in, g_cq, w_uq, g_ckv, w_ukv, g_out_a, g_out_b, w_o, g_ffn,
           w_r1, b_r1, w_r2, b_r2, w_e_gate, w_e_up, w_e_down, g_ple, w_ple_gate, w_ple_proj, g_out):
    t, d = h2.shape
    assert d == SUBLANES * LANES, "a token's activations must fill exactly one (8, 128) tile"
    w1, wqm, wk, wv, cosq, sinq, cosk, sink = _layer_weights(w_in, w_uq, w_ukv, seq)
    qkv, qf, kf, vb = _in_proj(h2, g_mix[None], w1, g_cq[None], wqm, g_ckv[None], wk, wv,
                               cosq, sinq, cosk, sink, seq)

    slopes = np.exp2(-8.0 * (np.arange(A_HEADS, dtype=np.float64) + 1.0) / A_HEADS)
    for window, dil in DILATED_BRANCHES:
        assert window // (2 * dil) == BAND_HALF
    bias = np.stack([_band_bias(slopes * dil * LOG2_E) for _, dil in DILATED_BRANCHES]).astype(np.float32)
    oa = _dilated(qkv.reshape(batch, seq, -1), bias, batch, seq).reshape(t, A_WIDTH)
    ob = _mla(qf, kf, vb, batch, seq).reshape(t, B_WIDTH)

    wr2, wrh, br = _router_weights(w_r1, b_r1, w_r2, b_r2)
    tri = (jnp.arange(PROJ_TILE)[:, None] > jnp.arange(PROJ_TILE)[None, :]).astype(BF16)
    h1, route, cnt = _post_mix(
        h2, oa, ob, g_out_a[None], g_out_b[None], w_o[:A_WIDTH].astype(BF16), w_o[A_WIDTH:].astype(BF16),
        g_ffn[None], wr2, wrh, br, tri)

    counts = cnt[0].astype(jnp.int32)
    items, offs = _work_items(counts, t)
    xs, pos = _dispatch(offs, route[:, ROUTE_KEY].astype(jnp.int32), route[:, ROUTE_RANK].astype(jnp.int32),
                        h1, route, g_ffn[None])
    w_gate_up = jnp.concatenate([w_e_gate, w_e_up], axis=-1).astype(BF16)
    ys = _experts(items, xs, w_gate_up, w_e_down.astype(BF16))
    return _finish(pos, ys, h1, p2, g_ple[None], w_ple_gate.astype(BF16),
                   w_ple_proj.astype(BF16), g_out[None])


def kernel(x, p, g_mix, w_in, g_cq, w_uq, g_ckv, w_ukv, g_out_a, g_out_b, w_o, g_ffn, w_r1, b_r1, w_r2,
           b_r2, w_e_gate, w_e_up, w_e_down, g_ple, w_ple_gate, w_ple_proj, g_final):
    batch, seq, d = x.shape
    depth = p.shape[0]
    assert depth == 1, "the final norm is fused into the single layer's last kernel"
    h = x.reshape(batch * seq, d)
    i = 0
    h = _layer(h, p[i].reshape(batch * seq, -1), batch, seq, g_mix[i], w_in[i], g_cq[i], w_uq[i],
               g_ckv[i], w_ukv[i], g_out_a[i], g_out_b[i], w_o[i], g_ffn[i], w_r1[i], b_r1[i], w_r2[i],
               b_r2[i], w_e_gate[i], w_e_up[i], w_e_down[i], g_ple[i], w_ple_gate[i], w_ple_proj[i],
               g_final)
    return h.reshape(batch, seq, d)
```

```python
import functools
import math

import jax
import jax.numpy as jnp
import numpy as np
from jax import lax
from jax.experimental import pallas as pl
from jax.experimental.pallas import tpu as pltpu

F32 = jnp.float32
BF16 = jnp.bfloat16

EPS = 1e-6
MASK_VALUE = -1e30
LOG2_E = math.log2(math.e)
LANES = 128
SUBLANES = 8

A_HEADS = 8
A_HEAD_DIM = 64
A_WIDTH = A_HEADS * A_HEAD_DIM
DILATED_BRANCHES = ((128, 1), (512, 4), (2048, 16))

B_HEADS = 8
B_NOPE = 64
B_ROPE = 32
B_VDIM = 64
B_WIDTH = B_HEADS * B_VDIM
ROPE_THETA = 10000.0

N_GROUPS = 4
EXPERTS_PER_GROUP = 8
N_EXPERTS = N_GROUPS * EXPERTS_PER_GROUP
N_PAIR_KEYS = N_EXPERTS * EXPERTS_PER_GROUP
N_PAIRS = N_GROUPS * EXPERTS_PER_GROUP * (EXPERTS_PER_GROUP - 1) // 2

VMEM_LIMIT = 56 * 1024 * 1024

PROJ_TILE = 512
TOKEN_TILE = 256
BAND_TILE = 128
BAND_HALF = 64
BAND_KEYS = 256
BAND_UNROLL = 16
DEINT = 4
MERGE_ROWS = 256
MLA_TQ = 256
MLA_Q_TILES = 4
EXPERT_ROWS = 256
REC_ROWS = 2 * SUBLANES
GATE_ROW = SUBLANES
DMA_ISSUE_UNROLL = 8
ROUTE_GATE_LO, ROUTE_GATE_HI, ROUTE_KEY, ROUTE_RANK = 0, 1, 2, 3


def _rms(x, g):
    return x * lax.rsqrt(jnp.mean(x * x, axis=-1, keepdims=True) + EPS) * g


def _params(*sem):
    return pltpu.CompilerParams(dimension_semantics=sem, vmem_limit_bytes=VMEM_LIMIT)


def _token_cols(ref, n_tokens, rows_per_token, lead=None):
    def rows(c):
        sl = pl.ds(c, n_tokens, stride=rows_per_token)
        return ref[sl, :] if lead is None else ref[lead, sl, :]
    return jnp.concatenate([rows(c) for c in range(SUBLANES)], axis=1)


def _swap_rope_halves(x, first_lane):
    half = B_ROPE // 2
    lane = lax.broadcasted_iota(jnp.int32, x.shape, 1)
    from_above = pltpu.roll(x, LANES - half, 1)
    from_below = pltpu.roll(x, half, 1)
    return jnp.where(lane < first_lane + half, from_above, from_below)


def _in_proj_kernel(x_ref, gmix_ref, w1_ref, gcq_ref, wqm_ref, gckv_ref, wk_ref, wv_ref,
                    cosq_ref, sinq_ref, cosk_ref, sink_ref,
                    qkv_ref, qf_ref, kf_ref, vb_ref, *, dq, dkv):
    a = _rms(x_ref[...], gmix_ref[...]).astype(BF16)
    o = 3 * A_WIDTH
    proj = jnp.dot(a, w1_ref[:, o:], preferred_element_type=F32)
    cq = _rms(proj[:, 0:dq], gcq_ref[...]).astype(BF16)
    ckv = _rms(proj[:, dq:dq + dkv], gckv_ref[...]).astype(BF16)
    kr = proj[:, dq + dkv:dq + dkv + LANES]

    qm = jnp.dot(cq, wqm_ref[...], preferred_element_type=F32)
    cosq = cosq_ref[...]
    sinq = sinq_ref[...]
    for h in range(B_HEADS):
        sl = slice(h * LANES, (h + 1) * LANES)
        qf_ref[:, sl] = (qm[:, sl] * cosq + _swap_rope_halves(qm[:, sl], B_NOPE) * sinq).astype(BF16)

    krot = kr * cosk_ref[...] + _swap_rope_halves(kr, 0) * sink_ref[...]
    krot = pltpu.roll(krot, B_NOPE, 1)
    kf = jnp.dot(ckv, wk_ref[...], preferred_element_type=F32)
    for h in range(B_HEADS):
        sl = slice(h * LANES, (h + 1) * LANES)
        kf_ref[:, sl] = (kf[:, sl] + krot).astype(BF16)
    vb = jnp.dot(ckv, wv_ref[...], preferred_element_type=F32)
    pair_lane = lax.broadcasted_iota(jnp.int32, vb.shape, 1) % (2 * LANES)
    ones_half = (pair_lane >= B_VDIM) & (pair_lane < 2 * LANES - B_VDIM)
    vb_ref[...] = jnp.where(ones_half, 1.0, vb).astype(BF16)
    qkv = jnp.dot(a, w1_ref[:, :o], preferred_element_type=F32)
    qkv_ref[:, :A_WIDTH] = qkv[:, :A_WIDTH] * LOG2_E
    qkv_ref[:, A_WIDTH:] = qkv[:, A_WIDTH:]


def _in_proj(x2, gmix, w1, gcq, wqm, gckv, wk, wv, cosq, sinq, cosk, sink, seq):
    t, d = x2.shape
    tm = PROJ_TILE
    n_pos = seq // tm
    dq, dkv = wqm.shape[0], wk.shape[0]
    full = lambda arr: pl.BlockSpec(arr.shape, lambda i: (0, 0))
    tab = pl.BlockSpec((tm, LANES), lambda i: (i % n_pos, 0))
    row = lambda w: pl.BlockSpec((tm, w), lambda i: (i, 0))
    outs = [(3 * A_WIDTH, F32)] + [(B_HEADS * LANES, BF16)] * 3
    return pl.pallas_call(
        functools.partial(_in_proj_kernel, dq=dq, dkv=dkv),
        grid=(t // tm,),
        in_specs=[row(d), full(gmix), full(w1), full(gcq), full(wqm), full(gckv),
                  full(wk), full(wv), tab, tab, tab, tab],
        out_specs=[row(w) for w, _ in outs],
        out_shape=[jax.ShapeDtypeStruct((t, w), dt) for w, dt in outs],
        compiler_params=_params("parallel"),
        name="in_proj",
    )(x2, gmix, w1, gcq, wqm, gckv, wk, wv, cosq, sinq, cosk, sink)


def _band_bias(slopes_scaled):
    qi = np.arange(BAND_TILE)[:, None]
    kc = np.arange(BAND_KEYS)[None, :]
    variants = []
    for shift in (0, BAND_HALF, BAND_KEYS - BAND_TILE):
        rel = np.abs(kc - shift - qi).astype(np.float64)
        bias = -slopes_scaled[:, None, None] * rel[None]
        variants.append(np.where(rel[None] <= BAND_HALF, bias, MASK_VALUE))
    return np.stack(variants, axis=1)


def _dilated_kernel(q_ref, k_ref, v_ref, bias_ref, out_ref, pv_scr, max_scr, den_scr, deint_scr, *, seq):
    lane = lax.broadcasted_iota(jnp.int32, (BAND_TILE, LANES), 1)
    low_half = lane < A_HEAD_DIM

    sub_len = seq // DEINT
    for a, src in enumerate((q_ref, k_ref, v_ref)):
        for c in range(DEINT):
            deint_scr[a, c * sub_len:(c + 1) * sub_len, :] = src[pl.ds(c, sub_len, stride=DEINT), :]

    for n, (_, dil) in enumerate(DILATED_BRANCHES):
        cls_len = seq // dil
        tiles_per_class = cls_len // BAND_TILE

        def tile(it, carry, n=n, dil=dil, cls_len=cls_len, tiles_per_class=tiles_per_class):
            r = it // tiles_per_class
            l0 = (it % tiles_per_class) * BAND_TILE
            start = jnp.clip(l0 - BAND_HALF, 0, cls_len - BAND_KEYS)
            variant = jnp.where(l0 == 0, 0, jnp.where(l0 == cls_len - BAND_TILE, 2, 1))
            if dil == 1:
                q_rows = pl.ds(pl.multiple_of(l0, BAND_TILE), BAND_TILE)
                k_rows = pl.ds(pl.multiple_of(start, BAND_HALF), BAND_KEYS)
                qp, kp, vp = q_ref[q_rows, :], k_ref[k_rows, :], v_ref[k_rows, :]
            else:
                q_rows = pl.ds(r + dil * l0, BAND_TILE, stride=dil)
                step = dil // DEINT
                base = (r % DEINT) * sub_len + r // DEINT
                if step == 1:
                    q_src = pl.ds(pl.multiple_of(base + l0, BAND_HALF), BAND_TILE)
                    k_src = pl.ds(pl.multiple_of(base + start, BAND_HALF), BAND_KEYS)
                else:
                    q_src = pl.ds(base + step * l0, BAND_TILE, stride=step)
                    k_src = pl.ds(base + step * start, BAND_KEYS, stride=step)
                qp, kp, vp = deint_scr[0, q_src, :], deint_scr[1, k_src, :], deint_scr[2, k_src, :]
            qp, kp, vp = qp.astype(BF16), kp.astype(BF16), vp.astype(BF16)
            pvs, maxes, dens = [], [], []
            for hh in range(2):
                qh = jnp.where(low_half if hh == 0 else ~low_half, qp, jnp.zeros_like(qp))
                s = lax.dot_general(qh, kp, (((1,), (1,)), ((), ())), preferred_element_type=F32)
                s = s + bias_ref[n, hh, variant]
                m = jnp.max(s, axis=-1, keepdims=True)
                p = jnp.exp2(s - m)
                dens.append(jnp.sum(p, axis=-1, keepdims=True))
                maxes.append(m)
                pvs.append(jnp.dot(p.astype(BF16), vp, preferred_element_type=F32))
            pv_scr[n, q_rows, :] = jnp.where(low_half, pvs[0], pvs[1])
            max_scr[n, q_rows, :] = jnp.where(low_half, maxes[0], maxes[1])
            den_scr[n, q_rows, :] = jnp.where(low_half, dens[0], dens[1])
            return carry

        lax.fori_loop(0, seq // BAND_TILE, tile, 0, unroll=BAND_UNROLL)

    def merge(c, carry):
        rows = pl.ds(pl.multiple_of(c * MERGE_ROWS, MERGE_ROWS), MERGE_ROWS)
        maxes = [max_scr[n, rows, :] for n in range(len(DILATED_BRANCHES))]
        top = functools.reduce(jnp.maximum, maxes)
        scales = [jnp.exp2(m - top) for m in maxes]
        num = sum(a * pv_scr[n, rows, :] for n, a in enumerate(scales))
        den = sum(a * den_scr[n, rows, :] for n, a in enumerate(scales))
        out_ref[rows, :] = (num / den).astype(BF16)
        return carry

    lax.fori_loop(0, seq // MERGE_ROWS, merge, 0)


def _dilated(qkv, bias, batch, seq):
    for _, dil in DILATED_BRANCHES:
        assert seq // dil >= BAND_KEYS and (seq // dil) % BAND_TILE == 0
        assert dil == 1 or dil % DEINT == 0
    pairs = A_HEADS // 2
    n_br = len(DILATED_BRANCHES)
    col = lambda base: pl.BlockSpec((None, seq, LANES), lambda b, hp: (b, 0, base + hp))
    return pl.pallas_call(
        functools.partial(_dilated_kernel, seq=seq),
        grid=(batch, pairs),
        in_specs=[col(0), col(pairs), col(2 * pairs),
                  pl.BlockSpec((n_br, 2, 3, BAND_TILE, BAND_KEYS), lambda b, hp: (0, hp, 0, 0, 0))],
        out_specs=col(0),
        out_shape=jax.ShapeDtypeStruct((batch, seq, A_WIDTH), BF16),
        scratch_shapes=[pltpu.VMEM((n_br, seq, LANES), F32)] * 3 + [pltpu.VMEM((3, seq, LANES), F32)],
        compiler_params=_params("parallel", "parallel"),
        name="dilated",
    )(qkv, qkv, qkv, bias)


def _mla_kernel(q_ref, k_ref, v_ref, o_ref):
    lane = lax.broadcasted_iota(jnp.int32, (MLA_TQ, LANES), 1)
    def qk(chain):
        qt, hh = divmod(chain, 2)
        rows = slice(qt * MLA_TQ, (qt + 1) * MLA_TQ)
        sl = slice(hh * LANES, (hh + 1) * LANES)
        return lax.dot_general(q_ref[rows, sl], k_ref[:, sl], (((1,), (1,)), ((), ())),
                               preferred_element_type=F32)

    def softmax_pv(s, hh):
        p = jnp.exp2(s - jnp.max(s, axis=-1, keepdims=True)).astype(BF16)
        pv = jnp.dot(p, v_ref[:, hh * LANES:(hh + 1) * LANES], preferred_element_type=F32)
        return pv / pltpu.roll(pv, B_VDIM, 1)

    n_chains = 2 * MLA_Q_TILES
    outs = []
    s_next = qk(0)
    for chain in range(n_chains):
        s = s_next
        if chain + 1 < n_chains:
            s_next = qk(chain + 1)
        outs.append(softmax_pv(s, chain % 2))
    for qt in range(MLA_Q_TILES):
        rows = slice(qt * MLA_TQ, (qt + 1) * MLA_TQ)
        o_ref[rows, :] = jnp.where(lane < B_VDIM, outs[2 * qt], outs[2 * qt + 1]).astype(BF16)


def _mla(qf, kf, vb, batch, seq):
    pairs = B_HEADS // 2
    rows = MLA_TQ * MLA_Q_TILES
    return pl.pallas_call(
        _mla_kernel,
        grid=(batch, pairs, seq // rows),
        in_specs=[pl.BlockSpec((None, rows, 2 * LANES), lambda b, hp, i: (b, i, hp)),
                  pl.BlockSpec((None, seq, 2 * LANES), lambda b, hp, i: (b, 0, hp)),
                  pl.BlockSpec((None, seq, 2 * LANES), lambda b, hp, i: (b, 0, hp))],
        out_specs=pl.BlockSpec((None, rows, LANES), lambda b, hp, i: (b, i, hp)),
        out_shape=jax.ShapeDtypeStruct((batch, seq, B_WIDTH), BF16),
        compiler_params=_params("parallel", "parallel", "parallel"),
        name="mla",
    )(qf.reshape(batch, seq, -1), kf.reshape(batch, seq, -1), vb.reshape(batch, seq, -1))


def _post_mix_kernel(x_ref, oa_ref, ob_ref, ga_ref, gb_ref, woa_ref, wob_ref, gffn_ref,
                     wr2_ref, wrh_ref, br_ref, tri_ref,
                     h_ref, route_ref, cnt_ref, carry_ref):
    i = pl.program_id(0)

    @pl.when(i == 0)
    def _():
        carry_ref[...] = jnp.zeros_like(carry_ref)

    na = _rms(oa_ref[...].astype(F32), ga_ref[...]).astype(BF16)
    nb = _rms(ob_ref[...].astype(F32), gb_ref[...]).astype(BF16)
    h = x_ref[...] + jnp.dot(na, woa_ref[...], preferred_element_type=F32)
    h = h + jnp.dot(nb, wob_ref[...], preferred_element_type=F32)
    h_ref[...] = h
    m = _rms(h, gffn_ref[...])
    tm = m.shape[0]

    m_hi = m.astype(BF16)
    m_lo = (m - m_hi.astype(F32)).astype(BF16)
    part = jnp.dot(m_hi, wr2_ref[...], preferred_element_type=F32)
    logits = part[:, :LANES] + part[:, LANES:] + br_ref[...]
    logits = logits + jnp.dot(m_lo, wrh_ref[...], preferred_element_type=F32)

    lane_i = lax.broadcasted_iota(jnp.int32, (tm, LANES), 1)
    lane = lane_i.astype(F32)
    big = float(LANES)
    is_group = (lane_i >= N_EXPERTS) & (lane_i < N_EXPERTS + N_GROUPS)
    lg = jnp.where(is_group, logits, -jnp.inf)
    g_max = jnp.max(lg, axis=-1, keepdims=True)
    g_idx = jnp.min(jnp.where(lg == g_max, lane - N_EXPERTS, big), axis=-1, keepdims=True)
    g_val = 1.0 / jnp.sum(jnp.exp(lg - g_max), axis=-1, keepdims=True)

    group_of_lane = (lane_i // EXPERTS_PER_GROUP).astype(F32)
    in_group = (lane_i < N_EXPERTS) & (group_of_lane == g_idx)
    le = jnp.where(in_group, logits, -jnp.inf)
    e_max = jnp.max(le, axis=-1, keepdims=True)
    ee = jnp.exp(le - e_max)
    p2 = jnp.where(in_group, ee / jnp.sum(ee, axis=-1, keepdims=True), -1.0)
    v1 = jnp.max(p2, axis=-1, keepdims=True)
    i1 = jnp.min(jnp.where(p2 == v1, lane, big), axis=-1, keepdims=True)
    p2b = jnp.where(lane == i1, -1.0, p2)
    v2 = jnp.max(p2b, axis=-1, keepdims=True)
    i2 = jnp.min(jnp.where(p2b == v2, lane, big), axis=-1, keepdims=True)
    norm = g_val / (v1 + v2)
    gate1 = v1 * norm
    gate2 = v2 * norm

    first_is_lo = i1 < i2
    e_lo = jnp.where(first_is_lo, i1, i2)
    e_hi = jnp.where(first_is_lo, i2, i1)
    g_lo = jnp.where(first_is_lo, gate1, gate2)
    g_hi = jnp.where(first_is_lo, gate2, gate1)
    key = e_lo * EXPERTS_PER_GROUP + (e_hi - g_idx * EXPERTS_PER_GROUP)

    key_lane = lax.broadcasted_iota(jnp.int32, (tm, N_PAIR_KEYS), 1).astype(F32)
    onehot = key_lane == key
    prefix = jnp.dot(tri_ref[...], onehot.astype(BF16), preferred_element_type=F32)
    carry = carry_ref[...]
    rank = jnp.sum(jnp.where(onehot, carry + prefix, 0.0), axis=-1, keepdims=True)
    carry = carry + jnp.sum(onehot.astype(F32), axis=0, keepdims=True)
    carry_ref[...] = carry
    cnt_ref[...] = jnp.broadcast_to(carry, cnt_ref.shape)

    route = jnp.where(lane_i == ROUTE_GATE_LO, g_lo, jnp.where(lane_i == ROUTE_GATE_HI, g_hi, 0.0))
    route_ref[...] = jnp.where(lane_i == ROUTE_KEY, key, jnp.where(lane_i == ROUTE_RANK, rank, route))


def _post_mix(x2, oa, ob, ga, gb, woa, wob, gffn, wr2, wrh, br, tri):
    t, d = x2.shape
    tm = PROJ_TILE
    full = lambda arr: pl.BlockSpec(arr.shape, lambda i: (0, 0))
    row = lambda w: pl.BlockSpec((tm, w), lambda i: (i, 0))
    return pl.pallas_call(
        _post_mix_kernel,
        grid=(t // tm,),
        in_specs=[row(d), row(A_WIDTH), row(B_WIDTH), full(ga), full(gb), full(woa), full(wob),
                  full(gffn), full(wr2), full(wrh), full(br), full(tri)],
        out_specs=[row(d), row(LANES), pl.BlockSpec((SUBLANES, N_PAIR_KEYS), lambda i: (0, 0))],
        out_shape=[jax.ShapeDtypeStruct((t, d), F32),
                   jax.ShapeDtypeStruct((t, LANES), F32),
                   jax.ShapeDtypeStruct((SUBLANES, N_PAIR_KEYS), F32)],
        scratch_shapes=[pltpu.VMEM((1, N_PAIR_KEYS), F32)],
        compiler_params=_params("arbitrary"),
        name="post_mix",
    )(x2, oa, ob, ga, gb, woa, wob, gffn, wr2, wrh, br, tri)


def _rows_copy(src, src_token, dst, dst_token, rows_per_token, sem):
    src_rows = pl.ds(pl.multiple_of(src_token * rows_per_token, rows_per_token), rows_per_token)
    dst_rows = pl.ds(pl.multiple_of(dst_token * rows_per_token, rows_per_token), rows_per_token)
    return pltpu.make_async_copy(src.at[src_rows, :], dst.at[dst_rows, :], sem)


def _wait_all_rows(vmem_buf, hbm_ref, n_rows, sem, *, to_hbm):
    hbm_rows = hbm_ref.at[pl.ds(0, n_rows), :]
    src, dst = (vmem_buf, hbm_rows) if to_hbm else (hbm_rows, vmem_buf)
    pltpu.make_async_copy(src, dst, sem).wait()


def _with_static_slot(slot, body):
    for s in range(2):
        @pl.when(slot == s)
        def _(s=s):
            body(s)


def _dispatch_kernel(offs_ref, key_ref, rank_ref, h_ref, route_ref, gffn_ref,
                     xs_ref, pos_ref, rec_a, rec_b, rec_c, sems, *, n_tiles):
    i = pl.program_id(0)
    tm = h_ref.shape[0]
    recs = (rec_a, rec_b, rec_c)

    def drain(b):
        _wait_all_rows(recs[b], xs_ref, tm * REC_ROWS, sems.at[b], to_hbm=True)

    def build(b):
        m = _rms(h_ref[...], gffn_ref[...])
        for c in range(SUBLANES):
            recs[b][pl.ds(c, tm, stride=REC_ROWS), :] = m[:, c * LANES:(c + 1) * LANES]
        recs[b][pl.ds(GATE_ROW, tm, stride=REC_ROWS), :] = route_ref[...]

    def scatter(b):
        for j in range(tm):
            pos = offs_ref[key_ref[j]] + rank_ref[j]
            pos_ref[j] = pos
            _rows_copy(recs[b], j, xs_ref, pos, REC_ROWS, sems.at[b]).start()

    for b in range(3):
        prev = (b + 2) % 3

        @pl.when(i % 3 == b)
        def _(b=b, prev=prev):
            @pl.when(i >= 3)
            def _():
                drain(b)

            @pl.when(i < 3)
            def _():
                recs[b][...] = jnp.zeros(recs[b].shape, F32)

            @pl.when((i >= 1) & (i < n_tiles))
            def _():
                build(b)
                scatter(prev)

            if b == 0:
                @pl.when(i == 0)
                def _():
                    build(b)

            if b == n_tiles % 3:
                @pl.when(i == n_tiles)
                def _():
                    scatter(prev)
                    drain(prev)
                    drain((prev + 2) % 3)


def _dispatch(offs, key, rank, h, route, gffn):
    t, d = h.shape
    tm = TOKEN_TILE
    n_tiles = t // tm
    assert n_tiles >= 3
    built = lambda i, of: (jnp.minimum(i, n_tiles - 1), 0)
    scattered = lambda i, of: (jnp.maximum(i - 1, 0),)
    grid_spec = pltpu.PrefetchScalarGridSpec(
        num_scalar_prefetch=1,
        grid=(n_tiles + 1,),
        in_specs=[pl.BlockSpec((tm,), scattered, memory_space=pltpu.SMEM),
                  pl.BlockSpec((tm,), scattered, memory_space=pltpu.SMEM),
                  pl.BlockSpec((tm, d), built),
                  pl.BlockSpec((tm, LANES), built),
                  pl.BlockSpec(gffn.shape, lambda i, of: (0, 0))],
        out_specs=[pl.BlockSpec(memory_space=pl.ANY),
                   pl.BlockSpec((tm,), scattered, memory_space=pltpu.SMEM)],
        scratch_shapes=[pltpu.VMEM((tm * REC_ROWS, LANES), F32)] * 3 + [pltpu.SemaphoreType.DMA((3,))],
    )
    return pl.pallas_call(
        functools.partial(_dispatch_kernel, n_tiles=n_tiles),
        grid_spec=grid_spec,
        out_shape=[jax.ShapeDtypeStruct((t * REC_ROWS, LANES), F32), jax.ShapeDtypeStruct((t,), jnp.int32)],
        compiler_params=_params("arbitrary"),
        name="dispatch",
    )(offs, key, rank, h, route, gffn)


def _ffn(x, wgu_ref, wd_ref):
    ff = wd_ref.shape[1]
    gu = jnp.dot(x, wgu_ref[0], preferred_element_type=F32)
    g, u = gu[:, :ff], gu[:, ff:]
    hid = (g * (1.0 / (1.0 + jnp.exp(-g))) * u).astype(BF16)
    return jnp.dot(hid, wd_ref[0], preferred_element_type=F32)


ITEM_VALID, ITEM_FIRST, ITEM_LAST = 1, 2, 4


def _experts_kernel(tile_ref, lo_ref, hi_ref, row0_ref, row1_ref, flags_ref,
                    xs_ref, wgu_lo, wd_lo, wgu_hi, wd_hi, ys_ref, x_scr, gate_scr, y_scr):
    w = pl.program_id(0)
    rows_n = EXPERT_ROWS
    flags = flags_ref[w]

    @pl.when((flags & ITEM_VALID) != 0)
    def _():
        @pl.when((flags & ITEM_FIRST) != 0)
        def _():
            x_scr[...] = _token_cols(xs_ref, rows_n, REC_ROWS).astype(BF16)
            gate_scr[...] = xs_ref[pl.ds(GATE_ROW, rows_n, stride=REC_ROWS), :]

        x = x_scr[...]
        gates = gate_scr[...]
        y = (gates[:, ROUTE_GATE_LO:ROUTE_GATE_LO + 1] * _ffn(x, wgu_lo, wd_lo)
             + gates[:, ROUTE_GATE_HI:ROUTE_GATE_HI + 1] * _ffn(x, wgu_hi, wd_hi))
        rows = tile_ref[w] * rows_n + lax.broadcasted_iota(jnp.int32, (rows_n, 1), 0)
        mine = (rows >= row0_ref[w]) & (rows < row1_ref[w])

        pltpu.store(y_scr, y, mask=jnp.broadcast_to(mine, y.shape))

        @pl.when((flags & ITEM_LAST) != 0)
        def _():
            for c in range(SUBLANES):
                ys_ref[pl.ds(c, rows_n, stride=SUBLANES), :] = y_scr[:, c * LANES:(c + 1) * LANES]


def _experts(items, xs, wgu, wd):
    n_tok = xs.shape[0] // REC_ROWS
    d, ff = wd.shape[2], wd.shape[1]
    n_items = items[0].shape[0]
    lo = lambda w, ti, lo_e, hi_e, r0, r1, fl: (lo_e[w], 0, 0)
    hi = lambda w, ti, lo_e, hi_e, r0, r1, fl: (hi_e[w], 0, 0)
    tile = lambda w, ti, lo_e, hi_e, r0, r1, fl: (ti[w], 0)
    grid_spec = pltpu.PrefetchScalarGridSpec(
        num_scalar_prefetch=6,
        grid=(n_items,),
        in_specs=[pl.BlockSpec((EXPERT_ROWS * REC_ROWS, LANES), tile),
                  pl.BlockSpec((1, d, 2 * ff), lo), pl.BlockSpec((1, ff, d), lo),
                  pl.BlockSpec((1, d, 2 * ff), hi), pl.BlockSpec((1, ff, d), hi)],
        out_specs=pl.BlockSpec((EXPERT_ROWS * SUBLANES, LANES), tile),
        scratch_shapes=[pltpu.VMEM((EXPERT_ROWS, d), BF16), pltpu.VMEM((EXPERT_ROWS, LANES), F32),
                        pltpu.VMEM((EXPERT_ROWS, d), F32)],
    )
    return pl.pallas_call(
        _experts_kernel,
        grid_spec=grid_spec,
        out_shape=jax.ShapeDtypeStruct((n_tok * SUBLANES, LANES), F32),
        compiler_params=_params("arbitrary"),
        name="experts",
    )(*items, xs, wgu, wd, wgu, wd)


def _work_items(counts, n_rows):
    n_keys = counts.shape[0]
    n_tiles = n_rows // EXPERT_ROWS
    n_items = n_tiles + N_PAIRS - 1
    offs = jnp.concatenate([jnp.zeros((1,), jnp.int32), jnp.cumsum(counts)])
    first_tile = offs[:-1] // EXPERT_ROWS
    last_tile = (offs[1:] - 1) // EXPERT_ROWS
    per_key = jnp.where(counts > 0, last_tile - first_tile + 1, 0)
    item_end = jnp.cumsum(per_key)
    item_start = item_end - per_key
    w = jnp.arange(n_items, dtype=jnp.int32)
    valid = w < item_end[-1]
    key = jnp.sum((item_end[None, :] <= w[:, None]).astype(jnp.int32), axis=1)
    key = jnp.minimum(key, n_keys - 1)
    tile = first_tile[key] + (w - item_start[key])
    last_valid = jnp.maximum(item_end[-1] - 1, 0)
    tile = jnp.where(valid, tile, tile[last_valid]).astype(jnp.int32)
    key = jnp.where(valid, key, key[last_valid])
    changes = tile[1:] != tile[:-1]
    first = jnp.concatenate([jnp.ones((1,), bool), changes])
    last = jnp.concatenate([changes, jnp.ones((1,), bool)]) | (w == last_valid)
    flags = ITEM_VALID * valid + ITEM_FIRST * first + ITEM_LAST * last
    lo = key // EXPERTS_PER_GROUP
    hi = lo // EXPERTS_PER_GROUP * EXPERTS_PER_GROUP + key % EXPERTS_PER_GROUP
    as_i32 = lambda a: a.astype(jnp.int32)
    items = tuple(as_i32(a) for a in (tile, lo, hi, offs[key], offs[key + 1], flags))
    return items, as_i32(offs)


def _finish_kernel(pos_ref, pos_next_ref, ys_ref, h_ref, p_ref, gple_ref, wpg_ref, wpp_ref, gfin_ref,
                   out_ref, rows_ref, sems, *, n_steps):
    i = pl.program_id(0)
    tm = h_ref.shape[0]

    def gather(positions, s):
        def body(j, carry):
            _rows_copy(ys_ref, positions[j], rows_ref.at[s], j, SUBLANES, sems.at[s]).start()
            return carry
        lax.fori_loop(0, tm, body, 0, unroll=DMA_ISSUE_UNROLL)

    def step(s):
        @pl.when(i == 0)
        def _():
            gather(pos_ref, s)

        @pl.when(i + 1 < n_steps)
        def _():
            gather(pos_next_ref, 1 - s)

        _wait_all_rows(rows_ref.at[s], ys_ref, tm * SUBLANES, sems.at[s], to_hbm=False)
        h = h_ref[...] + _token_cols(rows_ref, tm, SUBLANES, lead=s)
        gate_in = _rms(h, gple_ref[...]).astype(BF16)
        z = jnp.dot(gate_in, wpg_ref[...], preferred_element_type=F32)
        gate = 1.0 / (1.0 + jnp.exp(-z))
        h = h + gate * jnp.dot(p_ref[...].astype(BF16), wpp_ref[...], preferred_element_type=F32)
        out_ref[...] = _rms(h, gfin_ref[...])

    _with_static_slot(i % 2, step)


def _finish(pos, ys, h, p2, gple, wpg, wpp, gfin):
    t, d = h.shape
    tm = TOKEN_TILE
    full = lambda arr: pl.BlockSpec(arr.shape, lambda i: (0, 0))
    row = lambda w: pl.BlockSpec((tm, w), lambda i: (i, 0))
    n_steps = t // tm
    return pl.pallas_call(
        functools.partial(_finish_kernel, n_steps=n_steps),
        grid=(n_steps,),
        in_specs=[pl.BlockSpec((tm,), lambda i: (i,), memory_space=pltpu.SMEM),
                  pl.BlockSpec((tm,), lambda i: (jnp.minimum(i + 1, n_steps - 1),), memory_space=pltpu.SMEM),
                  pl.BlockSpec(memory_space=pl.ANY), row(d), row(p2.shape[1]),
                  full(gple), full(wpg), full(wpp), full(gfin)],
        out_specs=row(d),
        out_shape=jax.ShapeDtypeStruct((t, d), F32),
        scratch_shapes=[pltpu.VMEM((2, tm * SUBLANES, LANES), F32), pltpu.SemaphoreType.DMA((2,))],
        compiler_params=_params("arbitrary"),
        name="finish",
    )(pos, pos, ys, h, p2, gple, wpg, wpp, gfin)


def _layer_weights(w_in, w_uq, w_ukv, seq):
    d = w_in.shape[0]
    dq, dkv = w_uq.shape[0], w_ukv.shape[0]
    o = 3 * A_WIDTH + dq + dkv
    pad = jnp.zeros((d, LANES - B_ROPE), F32)
    w_q = w_in[:, :A_WIDTH] * (A_HEAD_DIM ** -0.5)
    w1 = jnp.concatenate([w_q, w_in[:, A_WIDTH:o + B_ROPE], pad], axis=1).astype(BF16)

    uq = w_uq.reshape(dq, B_HEADS, B_NOPE + B_ROPE)
    zq = jnp.zeros((dq, B_HEADS, LANES - B_NOPE - B_ROPE), F32)
    wqm = jnp.concatenate([uq, zq], axis=-1).reshape(dq, B_HEADS * LANES).astype(BF16)

    ukv = w_ukv.reshape(dkv, B_HEADS, B_NOPE + B_VDIM)
    wk = jnp.concatenate([ukv[..., :B_NOPE], jnp.zeros((dkv, B_HEADS, LANES - B_NOPE), F32)],
                         axis=-1).reshape(dkv, B_HEADS * LANES).astype(BF16)
    zv = jnp.zeros((dkv, B_HEADS // 2, B_VDIM), F32)
    uv = ukv[..., B_NOPE:].reshape(dkv, B_HEADS // 2, 2, B_VDIM)
    wv = jnp.concatenate([uv[:, :, 0], zv, zv, uv[:, :, 1]], axis=-1).reshape(dkv, B_HEADS * LANES).astype(BF16)

    inv_freq = 1.0 / (ROPE_THETA ** (np.arange(0, B_ROPE, 2, dtype=np.float64) / B_ROPE))
    ang = np.arange(seq, dtype=np.float64)[:, None] * inv_freq[None, :]
    cos, sin = np.cos(ang), np.sin(ang)
    scale = (B_NOPE + B_ROPE) ** -0.5 * LOG2_E
    zr = np.zeros((seq, LANES - B_NOPE - B_ROPE))
    cosq = scale * np.concatenate([np.ones((seq, B_NOPE)), cos, cos, zr], axis=1)
    sinq = scale * np.concatenate([np.zeros((seq, B_NOPE)), -sin, sin, zr], axis=1)
    zk = np.zeros((seq, LANES - B_ROPE))
    cosk = np.concatenate([cos, cos, zk], axis=1)
    sink = np.concatenate([-sin, sin, zk], axis=1)
    f32 = lambda a: a.astype(np.float32)
    return w1, wqm, wk, wv, f32(cosq), f32(sinq), f32(cosk), f32(sink)


def _router_weights(w_r1, b_r1, w_r2, b_r2):
    d = w_r1.shape[0]
    w2 = jnp.transpose(w_r2, (1, 0, 2)).reshape(d, N_EXPERTS)
    padw = jnp.zeros((d, LANES - N_EXPERTS - N_GROUPS), F32)
    wr = jnp.concatenate([w2, w_r1, padw], axis=1)
    wr_hi = wr.astype(BF16)
    wr_lo = (wr - wr_hi.astype(F32)).astype(BF16)
    br = jnp.concatenate([b_r2.reshape(N_EXPERTS), b_r1, jnp.zeros((LANES - N_EXPERTS - N_GROUPS,), F32)])
    return jnp.concatenate([wr_hi, wr_lo], axis=1), wr_hi, br[None, :]


def _layer(h2, p2, batch, seq, g_mix, w_in, g_cq, w_uq, g_ckv, w_ukv, g_out_a, g_out_b, w_o, g_ffn,
           w_r1, b_r1, w_r2, b_r2, w_e_gate, w_e_up, w_e_down, g_ple, w_ple_gate, w_ple_proj, g_out):
    t, d = h2.shape
    assert d == SUBLANES * LANES, "a token's activations must fill exactly one (8, 128) tile"
    w1, wqm, wk, wv, cosq, sinq, cosk, sink = _layer_weights(w_in, w_uq, w_ukv, seq)
    qkv, qf, kf, vb = _in_proj(h2, g_mix[None], w1, g_cq[None], wqm, g_ckv[None], wk, wv,
                               cosq, sinq, cosk, sink, seq)

    slopes = np.exp2(-8.0 * (np.arange(A_HEADS, dtype=np.float64) + 1.0) / A_HEADS)
    for window, dil in DILATED_BRANCHES:
        assert window // (2 * dil) == BAND_HALF
    bias = np.stack([_band_bias(slopes * dil * LOG2_E) for _, dil in DILATED_BRANCHES]).astype(np.float32)
    oa = _dilated(qkv.reshape(batch, seq, -1), bias, batch, seq).reshape(t, A_WIDTH)
    ob = _mla(qf, kf, vb, batch, seq).reshape(t, B_WIDTH)

    wr2, wrh, br = _router_weights(w_r1, b_r1, w_r2, b_r2)
    tri = (jnp.arange(PROJ_TILE)[:, None] > jnp.arange(PROJ_TILE)[None, :]).astype(BF16)
    h1, route, cnt = _post_mix(
        h2, oa, ob, g_out_a[None], g_out_b[None], w_o[:A_WIDTH].astype(BF16), w_o[A_WIDTH:].astype(BF16),
        g_ffn[None], wr2, wrh, br, tri)

    counts = cnt[0].astype(jnp.int32)
    items, offs = _work_items(counts, t)
    xs, pos = _dispatch(offs, route[:, ROUTE_KEY].astype(jnp.int32), route[:, ROUTE_RANK].astype(jnp.int32),
                        h1, route, g_ffn[None])
    w_gate_up = jnp.concatenate([w_e_gate, w_e_up], axis=-1).astype(BF16)
    ys = _experts(items, xs, w_gate_up, w_e_down.astype(BF16))
    return _finish(pos, ys, h1, p2, g_ple[None], w_ple_gate.astype(BF16),
                   w_ple_proj.astype(BF16), g_out[None])


def kernel(x, p, g_mix, w_in, g_cq, w_uq, g_ckv, w_ukv, g_out_a, g_out_b, w_o, g_ffn, w_r1, b_r1, w_r2,
           b_r2, w_e_gate, w_e_up, w_e_down, g_ple, w_ple_gate, w_ple_proj, g_final):
    batch, seq, d = x.shape
    depth = p.shape[0]
    assert depth == 1, "the final norm is fused into the single layer's last kernel"
    h = x.reshape(batch * seq, d)
    i = 0
    h = _layer(h, p[i].reshape(batch * seq, -1), batch, seq, g_mix[i], w_in[i], g_cq[i], w_uq[i],
               g_ckv[i], w_ukv[i], g_out_a[i], g_out_b[i], w_o[i], g_ffn[i], w_r1[i], b_r1[i], w_r2[i],
               b_r2[i], w_e_gate[i], w_e_up[i], w_e_down[i], g_ple[i], w_ple_gate[i], w_ple_proj[i],
               g_final)
    return h.reshape(batch, seq, d)
```

```python
import functools
import math

import jax
import jax.numpy as jnp
import numpy as np
from jax import lax
from jax.experimental import pallas as pl
from jax.experimental.pallas import tpu as pltpu

F32 = jnp.float32
BF16 = jnp.bfloat16

EPS = 1e-6
MASK_VALUE = -1e30
LOG2_E = math.log2(math.e)
LANES = 128
SUBLANES = 8

A_HEADS = 8
A_HEAD_DIM = 64
A_WIDTH = A_HEADS * A_HEAD_DIM
DILATED_BRANCHES = ((128, 1), (512, 4), (2048, 16))

B_HEADS = 8
B_NOPE = 64
B_ROPE = 32
B_VDIM = 64
B_WIDTH = B_HEADS * B_VDIM
ROPE_THETA = 10000.0

N_GROUPS = 4
EXPERTS_PER_GROUP = 8
N_EXPERTS = N_GROUPS * EXPERTS_PER_GROUP
N_PAIR_KEYS = N_EXPERTS * EXPERTS_PER_GROUP
N_PAIRS = N_GROUPS * EXPERTS_PER_GROUP * (EXPERTS_PER_GROUP - 1) // 2

VMEM_LIMIT = 56 * 1024 * 1024

PROJ_TILE = 512
TOKEN_TILE = 256
BAND_TILE = 128
BAND_HALF = 64
BAND_KEYS = 256
BAND_UNROLL = 32
DEINT = 4
MERGE_ROWS = 256
MLA_TQ = 256
MLA_Q_TILES = 4
EXPERT_ROWS = 256
REC_ROWS = 2 * SUBLANES
GATE_ROW = SUBLANES
DMA_ISSUE_UNROLL = 8
ROUTE_GATE_LO, ROUTE_GATE_HI, ROUTE_KEY, ROUTE_RANK = 0, 1, 2, 3


def _rms(x, g):
    return x * lax.rsqrt(jnp.mean(x * x, axis=-1, keepdims=True) + EPS) * g


def _params(*sem):
    return pltpu.CompilerParams(dimension_semantics=sem, vmem_limit_bytes=VMEM_LIMIT)


def _token_cols(ref, n_tokens, rows_per_token, lead=None):
    def rows(c):
        sl = pl.ds(c, n_tokens, stride=rows_per_token)
        return ref[sl, :] if lead is None else ref[lead, sl, :]
    return jnp.concatenate([rows(c) for c in range(SUBLANES)], axis=1)


def _swap_rope_halves(x, first_lane):
    half = B_ROPE // 2
    lane = lax.broadcasted_iota(jnp.int32, x.shape, 1)
    from_above = pltpu.roll(x, LANES - half, 1)
    from_below = pltpu.roll(x, half, 1)
    return jnp.where(lane < first_lane + half, from_above, from_below)


def _in_proj_kernel(x_ref, gmix_ref, w1_ref, gcq_ref, wqm_ref, gckv_ref, wk_ref, wv_ref,
                    cosq_ref, sinq_ref, cosk_ref, sink_ref,
                    qkv_ref, qf_ref, kf_ref, vb_ref, *, dq, dkv):
    a = _rms(x_ref[...], gmix_ref[...]).astype(BF16)
    o = 3 * A_WIDTH
    proj = jnp.dot(a, w1_ref[:, o:], preferred_element_type=F32)
    cq = _rms(proj[:, 0:dq], gcq_ref[...]).astype(BF16)
    ckv = _rms(proj[:, dq:dq + dkv], gckv_ref[...]).astype(BF16)
    kr = proj[:, dq + dkv:dq + dkv + LANES]

    qm = jnp.dot(cq, wqm_ref[...], preferred_element_type=F32)
    cosq = cosq_ref[...]
    sinq = sinq_ref[...]
    for h in range(B_HEADS):
        sl = slice(h * LANES, (h + 1) * LANES)
        qf_ref[:, sl] = (qm[:, sl] * cosq + _swap_rope_halves(qm[:, sl], B_NOPE) * sinq).astype(BF16)

    krot = kr * cosk_ref[...] + _swap_rope_halves(kr, 0) * sink_ref[...]
    krot = pltpu.roll(krot, B_NOPE, 1)
    kf = jnp.dot(ckv, wk_ref[...], preferred_element_type=F32)
    for h in range(B_HEADS):
        sl = slice(h * LANES, (h + 1) * LANES)
        kf_ref[:, sl] = (kf[:, sl] + krot).astype(BF16)
    vb = jnp.dot(ckv, wv_ref[...], preferred_element_type=F32)
    pair_lane = lax.broadcasted_iota(jnp.int32, vb.shape, 1) % (2 * LANES)
    ones_half = (pair_lane >= B_VDIM) & (pair_lane < 2 * LANES - B_VDIM)
    vb_ref[...] = jnp.where(ones_half, 1.0, vb).astype(BF16)
    qkv = jnp.dot(a, w1_ref[:, :o], preferred_element_type=F32)
    qkv_ref[:, :A_WIDTH] = qkv[:, :A_WIDTH] * LOG2_E
    qkv_ref[:, A_WIDTH:] = qkv[:, A_WIDTH:]


def _in_proj(x2, gmix, w1, gcq, wqm, gckv, wk, wv, cosq, sinq, cosk, sink, seq):
    t, d = x2.shape
    tm = PROJ_TILE
    n_pos = seq // tm
    dq, dkv = wqm.shape[0], wk.shape[0]
    full = lambda arr: pl.BlockSpec(arr.shape, lambda i: (0, 0))
    tab = pl.BlockSpec((tm, LANES), lambda i: (i % n_pos, 0))
    row = lambda w: pl.BlockSpec((tm, w), lambda i: (i, 0))
    outs = [(3 * A_WIDTH, F32)] + [(B_HEADS * LANES, BF16)] * 3
    return pl.pallas_call(
        functools.partial(_in_proj_kernel, dq=dq, dkv=dkv),
        grid=(t // tm,),
        in_specs=[row(d), full(gmix), full(w1), full(gcq), full(wqm), full(gckv),
                  full(wk), full(wv), tab, tab, tab, tab],
        out_specs=[row(w) for w, _ in outs],
        out_shape=[jax.ShapeDtypeStruct((t, w), dt) for w, dt in outs],
        compiler_params=_params("parallel"),
        name="in_proj",
    )(x2, gmix, w1, gcq, wqm, gckv, wk, wv, cosq, sinq, cosk, sink)


def _band_bias(slopes_scaled):
    qi = np.arange(BAND_TILE)[:, None]
    kc = np.arange(BAND_KEYS)[None, :]
    variants = []
    for shift in (0, BAND_HALF, BAND_KEYS - BAND_TILE):
        rel = np.abs(kc - shift - qi).astype(np.float64)
        bias = -slopes_scaled[:, None, None] * rel[None]
        variants.append(np.where(rel[None] <= BAND_HALF, bias, MASK_VALUE))
    return np.stack(variants, axis=1)


def _dilated_kernel(q_ref, k_ref, v_ref, bias_ref, out_ref, pv_scr, max_scr, den_scr, deint_scr, *, seq):
    lane = lax.broadcasted_iota(jnp.int32, (BAND_TILE, LANES), 1)
    low_half = lane < A_HEAD_DIM

    sub_len = seq // DEINT
    for a, src in enumerate((q_ref, k_ref, v_ref)):
        for c in range(DEINT):
            deint_scr[a, c * sub_len:(c + 1) * sub_len, :] = src[pl.ds(c, sub_len, stride=DEINT), :]

    for n, (_, dil) in enumerate(DILATED_BRANCHES):
        cls_len = seq // dil
        tiles_per_class = cls_len // BAND_TILE

        def tile(it, carry, n=n, dil=dil, cls_len=cls_len, tiles_per_class=tiles_per_class):
            r = it // tiles_per_class
            l0 = (it % tiles_per_class) * BAND_TILE
            start = jnp.clip(l0 - BAND_HALF, 0, cls_len - BAND_KEYS)
            variant = jnp.where(l0 == 0, 0, jnp.where(l0 == cls_len - BAND_TILE, 2, 1))
            if dil == 1:
                q_rows = pl.ds(pl.multiple_of(l0, BAND_TILE), BAND_TILE)
                k_rows = pl.ds(pl.multiple_of(start, BAND_HALF), BAND_KEYS)
                qp, kp, vp = q_ref[q_rows, :], k_ref[k_rows, :], v_ref[k_rows, :]
            else:
                q_rows = pl.ds(r + dil * l0, BAND_TILE, stride=dil)
                step = dil // DEINT
                base = (r % DEINT) * sub_len + r // DEINT
                if step == 1:
                    q_src = pl.ds(pl.multiple_of(base + l0, BAND_HALF), BAND_TILE)
                    k_src = pl.ds(pl.multiple_of(base + start, BAND_HALF), BAND_KEYS)
                else:
                    q_src = pl.ds(base + step * l0, BAND_TILE, stride=step)
                    k_src = pl.ds(base + step * start, BAND_KEYS, stride=step)
                qp, kp, vp = deint_scr[0, q_src, :], deint_scr[1, k_src, :], deint_scr[2, k_src, :]
            qp, kp, vp = qp.astype(BF16), kp.astype(BF16), vp.astype(BF16)
            pvs, maxes, dens = [], [], []
            for hh in range(2):
                qh = jnp.where(low_half if hh == 0 else ~low_half, qp, jnp.zeros_like(qp))
                s = lax.dot_general(qh, kp, (((1,), (1,)), ((), ())), preferred_element_type=F32)
                s = s + bias_ref[n, hh, variant]
                m = jnp.max(s, axis=-1, keepdims=True)
                p = jnp.exp2(s - m)
                dens.append(jnp.sum(p, axis=-1, keepdims=True))
                maxes.append(m)
                pvs.append(jnp.dot(p.astype(BF16), vp, preferred_element_type=F32))
            pv_scr[n, q_rows, :] = jnp.where(low_half, pvs[0], pvs[1])
            max_scr[n, q_rows, :] = jnp.where(low_half, maxes[0], maxes[1])
            den_scr[n, q_rows, :] = jnp.where(low_half, dens[0], dens[1])
            return carry

        lax.fori_loop(0, seq // BAND_TILE, tile, 0, unroll=BAND_UNROLL)

    def merge(c, carry):
        rows = pl.ds(pl.multiple_of(c * MERGE_ROWS, MERGE_ROWS), MERGE_ROWS)
        maxes = [max_scr[n, rows, :] for n in range(len(DILATED_BRANCHES))]
        top = functools.reduce(jnp.maximum, maxes)
        scales = [jnp.exp2(m - top) for m in maxes]
        num = sum(a * pv_scr[n, rows, :] for n, a in enumerate(scales))
        den = sum(a * den_scr[n, rows, :] for n, a in enumerate(scales))
        out_ref[rows, :] = (num / den).astype(BF16)
        return carry

    lax.fori_loop(0, seq // MERGE_ROWS, merge, 0)


def _dilated(qkv, bias, batch, seq):
    for _, dil in DILATED_BRANCHES:
        assert seq // dil >= BAND_KEYS and (seq // dil) % BAND_TILE == 0
        assert dil == 1 or dil % DEINT == 0
    pairs = A_HEADS // 2
    n_br = len(DILATED_BRANCHES)
    col = lambda base: pl.BlockSpec((None, seq, LANES), lambda b, hp: (b, 0, base + hp))
    return pl.pallas_call(
        functools.partial(_dilated_kernel, seq=seq),
        grid=(batch, pairs),
        in_specs=[col(0), col(pairs), col(2 * pairs),
                  pl.BlockSpec((n_br, 2, 3, BAND_TILE, BAND_KEYS), lambda b, hp: (0, hp, 0, 0, 0))],
        out_specs=col(0),
        out_shape=jax.ShapeDtypeStruct((batch, seq, A_WIDTH), BF16),
        scratch_shapes=[pltpu.VMEM((n_br, seq, LANES), F32)] * 3 + [pltpu.VMEM((3, seq, LANES), F32)],
        compiler_params=_params("parallel", "parallel"),
        name="dilated",
    )(qkv, qkv, qkv, bias)


def _mla_kernel(q_ref, k_ref, v_ref, o_ref):
    lane = lax.broadcasted_iota(jnp.int32, (MLA_TQ, LANES), 1)
    def qk(chain):
        qt, hh = divmod(chain, 2)
        rows = slice(qt * MLA_TQ, (qt + 1) * MLA_TQ)
        sl = slice(hh * LANES, (hh + 1) * LANES)
        return lax.dot_general(q_ref[rows, sl], k_ref[:, sl], (((1,), (1,)), ((), ())),
                               preferred_element_type=F32)

    def softmax_pv(s, hh):
        p = jnp.exp2(s - jnp.max(s, axis=-1, keepdims=True)).astype(BF16)
        pv = jnp.dot(p, v_ref[:, hh * LANES:(hh + 1) * LANES], preferred_element_type=F32)
        return pv / pltpu.roll(pv, B_VDIM, 1)

    n_chains = 2 * MLA_Q_TILES
    outs = []
    s_next = qk(0)
    for chain in range(n_chains):
        s = s_next
        if chain + 1 < n_chains:
            s_next = qk(chain + 1)
        outs.append(softmax_pv(s, chain % 2))
    for qt in range(MLA_Q_TILES):
        rows = slice(qt * MLA_TQ, (qt + 1) * MLA_TQ)
        o_ref[rows, :] = jnp.where(lane < B_VDIM, outs[2 * qt], outs[2 * qt + 1]).astype(BF16)


def _mla(qf, kf, vb, batch, seq):
    pairs = B_HEADS // 2
    rows = MLA_TQ * MLA_Q_TILES
    return pl.pallas_call(
        _mla_kernel,
        grid=(batch, pairs, seq // rows),
        in_specs=[pl.BlockSpec((None, rows, 2 * LANES), lambda b, hp, i: (b, i, hp)),
                  pl.BlockSpec((None, seq, 2 * LANES), lambda b, hp, i: (b, 0, hp)),
                  pl.BlockSpec((None, seq, 2 * LANES), lambda b, hp, i: (b, 0, hp))],
        out_specs=pl.BlockSpec((None, rows, LANES), lambda b, hp, i: (b, i, hp)),
        out_shape=jax.ShapeDtypeStruct((batch, seq, B_WIDTH), BF16),
        compiler_params=_params("parallel", "parallel", "parallel"),
        name="mla",
    )(qf.reshape(batch, seq, -1), kf.reshape(batch, seq, -1), vb.reshape(batch, seq, -1))


def _post_mix_kernel(x_ref, oa_ref, ob_ref, ga_ref, gb_ref, woa_ref, wob_ref, gffn_ref,
                     wr2_ref, wrh_ref, br_ref, tri_ref,
                     h_ref, route_ref, cnt_ref, carry_ref):
    i = pl.program_id(0)

    @pl.when(i == 0)
    def _():
        carry_ref[...] = jnp.zeros_like(carry_ref)

    na = _rms(oa_ref[...].astype(F32), ga_ref[...]).astype(BF16)
    nb = _rms(ob_ref[...].astype(F32), gb_ref[...]).astype(BF16)
    h = x_ref[...] + jnp.dot(na, woa_ref[...], preferred_element_type=F32)
    h = h + jnp.dot(nb, wob_ref[...], preferred_element_type=F32)
    h_ref[...] = h
    m = _rms(h, gffn_ref[...])
    tm = m.shape[0]

    m_hi = m.astype(BF16)
    m_lo = (m - m_hi.astype(F32)).astype(BF16)
    part = jnp.dot(m_hi, wr2_ref[...], preferred_element_type=F32)
    logits = part[:, :LANES] + part[:, LANES:] + br_ref[...]
    logits = logits + jnp.dot(m_lo, wrh_ref[...], preferred_element_type=F32)

    lane_i = lax.broadcasted_iota(jnp.int32, (tm, LANES), 1)
    lane = lane_i.astype(F32)
    big = float(LANES)
    is_group = (lane_i >= N_EXPERTS) & (lane_i < N_EXPERTS + N_GROUPS)
    lg = jnp.where(is_group, logits, -jnp.inf)
    g_max = jnp.max(lg, axis=-1, keepdims=True)
    g_idx = jnp.min(jnp.where(lg == g_max, lane - N_EXPERTS, big), axis=-1, keepdims=True)
    g_val = 1.0 / jnp.sum(jnp.exp(lg - g_max), axis=-1, keepdims=True)

    group_of_lane = (lane_i // EXPERTS_PER_GROUP).astype(F32)
    in_group = (lane_i < N_EXPERTS) & (group_of_lane == g_idx)
    le = jnp.where(in_group, logits, -jnp.inf)
    e_max = jnp.max(le, axis=-1, keepdims=True)
    ee = jnp.exp(le - e_max)
    p2 = jnp.where(in_group, ee / jnp.sum(ee, axis=-1, keepdims=True), -1.0)
    v1 = jnp.max(p2, axis=-1, keepdims=True)
    i1 = jnp.min(jnp.where(p2 == v1, lane, big), axis=-1, keepdims=True)
    p2b = jnp.where(lane == i1, -1.0, p2)
    v2 = jnp.max(p2b, axis=-1, keepdims=True)
    i2 = jnp.min(jnp.where(p2b == v2, lane, big), axis=-1, keepdims=True)
    norm = g_val / (v1 + v2)
    gate1 = v1 * norm
    gate2 = v2 * norm

    first_is_lo = i1 < i2
    e_lo = jnp.where(first_is_lo, i1, i2)
    e_hi = jnp.where(first_is_lo, i2, i1)
    g_lo = jnp.where(first_is_lo, gate1, gate2)
    g_hi = jnp.where(first_is_lo, gate2, gate1)
    key = e_lo * EXPERTS_PER_GROUP + (e_hi - g_idx * EXPERTS_PER_GROUP)

    key_lane = lax.broadcasted_iota(jnp.int32, (tm, N_PAIR_KEYS), 1).astype(F32)
    onehot = key_lane == key
    prefix = jnp.dot(tri_ref[...], onehot.astype(BF16), preferred_element_type=F32)
    carry = carry_ref[...]
    rank = jnp.sum(jnp.where(onehot, carry + prefix, 0.0), axis=-1, keepdims=True)
    carry = carry + jnp.sum(onehot.astype(F32), axis=0, keepdims=True)
    carry_ref[...] = carry
    cnt_ref[...] = jnp.broadcast_to(carry, cnt_ref.shape)

    route = jnp.where(lane_i == ROUTE_GATE_LO, g_lo, jnp.where(lane_i == ROUTE_GATE_HI, g_hi, 0.0))
    route_ref[...] = jnp.where(lane_i == ROUTE_KEY, key, jnp.where(lane_i == ROUTE_RANK, rank, route))


def _post_mix(x2, oa, ob, ga, gb, woa, wob, gffn, wr2, wrh, br, tri):
    t, d = x2.shape
    tm = PROJ_TILE
    full = lambda arr: pl.BlockSpec(arr.shape, lambda i: (0, 0))
    row = lambda w: pl.BlockSpec((tm, w), lambda i: (i, 0))
    return pl.pallas_call(
        _post_mix_kernel,
        grid=(t // tm,),
        in_specs=[row(d), row(A_WIDTH), row(B_WIDTH), full(ga), full(gb), full(woa), full(wob),
                  full(gffn), full(wr2), full(wrh), full(br), full(tri)],
        out_specs=[row(d), row(LANES), pl.BlockSpec((SUBLANES, N_PAIR_KEYS), lambda i: (0, 0))],
        out_shape=[jax.ShapeDtypeStruct((t, d), F32),
                   jax.ShapeDtypeStruct((t, LANES), F32),
                   jax.ShapeDtypeStruct((SUBLANES, N_PAIR_KEYS), F32)],
        scratch_shapes=[pltpu.VMEM((1, N_PAIR_KEYS), F32)],
        compiler_params=_params("arbitrary"),
        name="post_mix",
    )(x2, oa, ob, ga, gb, woa, wob, gffn, wr2, wrh, br, tri)


def _rows_copy(src, src_token, dst, dst_token, rows_per_token, sem):
    src_rows = pl.ds(pl.multiple_of(src_token * rows_per_token, rows_per_token), rows_per_token)
    dst_rows = pl.ds(pl.multiple_of(dst_token * rows_per_token, rows_per_token), rows_per_token)
    return pltpu.make_async_copy(src.at[src_rows, :], dst.at[dst_rows, :], sem)


def _wait_all_rows(vmem_buf, hbm_ref, n_rows, sem, *, to_hbm):
    hbm_rows = hbm_ref.at[pl.ds(0, n_rows), :]
    src, dst = (vmem_buf, hbm_rows) if to_hbm else (hbm_rows, vmem_buf)
    pltpu.make_async_copy(src, dst, sem).wait()


def _with_static_slot(slot, body):
    for s in range(2):
        @pl.when(slot == s)
        def _(s=s):
            body(s)


def _dispatch_kernel(offs_ref, key_ref, rank_ref, h_ref, route_ref, gffn_ref,
                     xs_ref, pos_ref, rec_a, rec_b, rec_c, sems, *, n_tiles):
    i = pl.program_id(0)
    tm = h_ref.shape[0]
    recs = (rec_a, rec_b, rec_c)

    def drain(b):
        _wait_all_rows(recs[b], xs_ref, tm * REC_ROWS, sems.at[b], to_hbm=True)

    def build(b):
        m = _rms(h_ref[...], gffn_ref[...])
        for c in range(SUBLANES):
            recs[b][pl.ds(c, tm, stride=REC_ROWS), :] = m[:, c * LANES:(c + 1) * LANES]
        recs[b][pl.ds(GATE_ROW, tm, stride=REC_ROWS), :] = route_ref[...]

    def scatter(b):
        for j in range(tm):
            pos = offs_ref[key_ref[j]] + rank_ref[j]
            pos_ref[j] = pos
            _rows_copy(recs[b], j, xs_ref, pos, REC_ROWS, sems.at[b]).start()

    for b in range(3):
        prev = (b + 2) % 3

        @pl.when(i % 3 == b)
        def _(b=b, prev=prev):
            @pl.when(i >= 3)
            def _():
                drain(b)

            @pl.when(i < 3)
            def _():
                recs[b][...] = jnp.zeros(recs[b].shape, F32)

            @pl.when((i >= 1) & (i < n_tiles))
            def _():
                build(b)
                scatter(prev)

            if b == 0:
                @pl.when(i == 0)
                def _():
                    build(b)

            if b == n_tiles % 3:
                @pl.when(i == n_tiles)
                def _():
                    scatter(prev)
                    drain(prev)
                    drain((prev + 2) % 3)


def _dispatch(offs, key, rank, h, route, gffn):
    t, d = h.shape
    tm = TOKEN_TILE
    n_tiles = t // tm
    assert n_tiles >= 3
    built = lambda i, of: (jnp.minimum(i, n_tiles - 1), 0)
    scattered = lambda i, of: (jnp.maximum(i - 1, 0),)
    grid_spec = pltpu.PrefetchScalarGridSpec(
        num_scalar_prefetch=1,
        grid=(n_tiles + 1,),
        in_specs=[pl.BlockSpec((tm,), scattered, memory_space=pltpu.SMEM),
                  pl.BlockSpec((tm,), scattered, memory_space=pltpu.SMEM),
                  pl.BlockSpec((tm, d), built),
                  pl.BlockSpec((tm, LANES), built),
                  pl.BlockSpec(gffn.shape, lambda i, of: (0, 0))],
        out_specs=[pl.BlockSpec(memory_space=pl.ANY),
                   pl.BlockSpec((tm,), scattered, memory_space=pltpu.SMEM)],
        scratch_shapes=[pltpu.VMEM((tm * REC_ROWS, LANES), F32)] * 3 + [pltpu.SemaphoreType.DMA((3,))],
    )
    return pl.pallas_call(
        functools.partial(_dispatch_kernel, n_tiles=n_tiles),
        grid_spec=grid_spec,
        out_shape=[jax.ShapeDtypeStruct((t * REC_ROWS, LANES), F32), jax.ShapeDtypeStruct((t,), jnp.int32)],
        compiler_params=_params("arbitrary"),
        name="dispatch",
    )(offs, key, rank, h, route, gffn)


def _ffn(x, wg_ref, wu_ref, wd_ref):
    g = jnp.dot(x, wg_ref[0], preferred_element_type=F32)
    u = jnp.dot(x, wu_ref[0], preferred_element_type=F32)
    hid = (g * (1.0 / (1.0 + jnp.exp(-g))) * u).astype(BF16)
    return jnp.dot(hid, wd_ref[0], preferred_element_type=F32)


ITEM_VALID, ITEM_FIRST, ITEM_LAST = 1, 2, 4


def _experts_kernel(tile_ref, lo_ref, hi_ref, row0_ref, row1_ref, flags_ref,
                    xs_ref, wg_lo, wu_lo, wd_lo, wg_hi, wu_hi, wd_hi, ys_ref, x_scr, gate_scr, y_scr):
    w = pl.program_id(0)
    rows_n = EXPERT_ROWS
    flags = flags_ref[w]

    @pl.when((flags & ITEM_VALID) != 0)
    def _():
        @pl.when((flags & ITEM_FIRST) != 0)
        def _():
            x_scr[...] = _token_cols(xs_ref, rows_n, REC_ROWS).astype(BF16)
            gate_scr[...] = xs_ref[pl.ds(GATE_ROW, rows_n, stride=REC_ROWS), :]

        x = x_scr[...]
        gates = gate_scr[...]
        y = (gates[:, ROUTE_GATE_LO:ROUTE_GATE_LO + 1] * _ffn(x, wg_lo, wu_lo, wd_lo)
             + gates[:, ROUTE_GATE_HI:ROUTE_GATE_HI + 1] * _ffn(x, wg_hi, wu_hi, wd_hi))
        rows = tile_ref[w] * rows_n + lax.broadcasted_iota(jnp.int32, (rows_n, 1), 0)
        mine = (rows >= row0_ref[w]) & (rows < row1_ref[w])

        pltpu.store(y_scr, y, mask=jnp.broadcast_to(mine, y.shape))

        @pl.when((flags & ITEM_LAST) != 0)
        def _():
            for c in range(SUBLANES):
                ys_ref[pl.ds(c, rows_n, stride=SUBLANES), :] = y_scr[:, c * LANES:(c + 1) * LANES]


def _experts(items, xs, wg, wu, wd):
    n_tok = xs.shape[0] // REC_ROWS
    d, ff = wd.shape[2], wd.shape[1]
    up = lambda index_map: pl.BlockSpec((1, d, ff), index_map)
    down = lambda index_map: pl.BlockSpec((1, ff, d), index_map)
    n_items = items[0].shape[0]
    lo = lambda w, ti, lo_e, hi_e, r0, r1, fl: (lo_e[w], 0, 0)
    hi = lambda w, ti, lo_e, hi_e, r0, r1, fl: (hi_e[w], 0, 0)
    tile = lambda w, ti, lo_e, hi_e, r0, r1, fl: (ti[w], 0)
    grid_spec = pltpu.PrefetchScalarGridSpec(
        num_scalar_prefetch=6,
        grid=(n_items,),
        in_specs=[pl.BlockSpec((EXPERT_ROWS * REC_ROWS, LANES), tile),
                  up(lo), up(lo), down(lo), up(hi), up(hi), down(hi)],
        out_specs=pl.BlockSpec((EXPERT_ROWS * SUBLANES, LANES), tile),
        scratch_shapes=[pltpu.VMEM((EXPERT_ROWS, d), BF16), pltpu.VMEM((EXPERT_ROWS, LANES), F32),
                        pltpu.VMEM((EXPERT_ROWS, d), F32)],
    )
    return pl.pallas_call(
        _experts_kernel,
        grid_spec=grid_spec,
        out_shape=jax.ShapeDtypeStruct((n_tok * SUBLANES, LANES), F32),
        compiler_params=_params("arbitrary"),
        name="experts",
    )(*items, xs, wg, wu, wd, wg, wu, wd)


def _work_items(counts, n_rows):
    n_keys = counts.shape[0]
    n_tiles = n_rows // EXPERT_ROWS
    n_items = n_tiles + N_PAIRS - 1
    offs = jnp.concatenate([jnp.zeros((1,), jnp.int32), jnp.cumsum(counts)])
    first_tile = offs[:-1] // EXPERT_ROWS
    last_tile = (offs[1:] - 1) // EXPERT_ROWS
    per_key = jnp.where(counts > 0, last_tile - first_tile + 1, 0)
    item_end = jnp.cumsum(per_key)
    item_start = item_end - per_key
    w = jnp.arange(n_items, dtype=jnp.int32)
    valid = w < item_end[-1]
    key = jnp.sum((item_end[None, :] <= w[:, None]).astype(jnp.int32), axis=1)
    key = jnp.minimum(key, n_keys - 1)
    tile = first_tile[key] + (w - item_start[key])
    last_valid = jnp.maximum(item_end[-1] - 1, 0)
    tile = jnp.where(valid, tile, tile[last_valid]).astype(jnp.int32)
    key = jnp.where(valid, key, key[last_valid])
    changes = tile[1:] != tile[:-1]
    first = jnp.concatenate([jnp.ones((1,), bool), changes])
    last = jnp.concatenate([changes, jnp.ones((1,), bool)]) | (w == last_valid)
    flags = ITEM_VALID * valid + ITEM_FIRST * first + ITEM_LAST * last
    lo = key // EXPERTS_PER_GROUP
    hi = lo // EXPERTS_PER_GROUP * EXPERTS_PER_GROUP + key % EXPERTS_PER_GROUP
    as_i32 = lambda a: a.astype(jnp.int32)
    items = tuple(as_i32(a) for a in (tile, lo, hi, offs[key], offs[key + 1], flags))
    return items, as_i32(offs)


def _finish_kernel(pos_ref, pos_next_ref, ys_ref, h_ref, p_ref, gple_ref, wpg_ref, wpp_ref, gfin_ref,
                   out_ref, rows_ref, sems, *, n_steps):
    i = pl.program_id(0)
    tm = h_ref.shape[0]

    def gather(positions, s):
        def body(j, carry):
            _rows_copy(ys_ref, positions[j], rows_ref.at[s], j, SUBLANES, sems.at[s]).start()
            return carry
        lax.fori_loop(0, tm, body, 0, unroll=DMA_ISSUE_UNROLL)

    def step(s):
        @pl.when(i == 0)
        def _():
            gather(pos_ref, s)

        @pl.when(i + 1 < n_steps)
        def _():
            gather(pos_next_ref, 1 - s)

        _wait_all_rows(rows_ref.at[s], ys_ref, tm * SUBLANES, sems.at[s], to_hbm=False)
        h = h_ref[...] + _token_cols(rows_ref, tm, SUBLANES, lead=s)
        gate_in = _rms(h, gple_ref[...]).astype(BF16)
        z = jnp.dot(gate_in, wpg_ref[...], preferred_element_type=F32)
        gate = 1.0 / (1.0 + jnp.exp(-z))
        h = h + gate * jnp.dot(p_ref[...].astype(BF16), wpp_ref[...], preferred_element_type=F32)
        out_ref[...] = _rms(h, gfin_ref[...])

    _with_static_slot(i % 2, step)


def _finish(pos, ys, h, p2, gple, wpg, wpp, gfin):
    t, d = h.shape
    tm = TOKEN_TILE
    full = lambda arr: pl.BlockSpec(arr.shape, lambda i: (0, 0))
    row = lambda w: pl.BlockSpec((tm, w), lambda i: (i, 0))
    n_steps = t // tm
    return pl.pallas_call(
        functools.partial(_finish_kernel, n_steps=n_steps),
        grid=(n_steps,),
        in_specs=[pl.BlockSpec((tm,), lambda i: (i,), memory_space=pltpu.SMEM),
                  pl.BlockSpec((tm,), lambda i: (jnp.minimum(i + 1, n_steps - 1),), memory_space=pltpu.SMEM),
                  pl.BlockSpec(memory_space=pl.ANY), row(d), row(p2.shape[1]),
                  full(gple), full(wpg), full(wpp), full(gfin)],
        out_specs=row(d),
        out_shape=jax.ShapeDtypeStruct((t, d), F32),
        scratch_shapes=[pltpu.VMEM((2, tm * SUBLANES, LANES), F32), pltpu.SemaphoreType.DMA((2,))],
        compiler_params=_params("arbitrary"),
        name="finish",
    )(pos, pos, ys, h, p2, gple, wpg, wpp, gfin)


def _layer_weights(w_in, w_uq, w_ukv, seq):
    d = w_in.shape[0]
    dq, dkv = w_uq.shape[0], w_ukv.shape[0]
    o = 3 * A_WIDTH + dq + dkv
    pad = jnp.zeros((d, LANES - B_ROPE), F32)
    w_q = w_in[:, :A_WIDTH] * (A_HEAD_DIM ** -0.5)
    w1 = jnp.concatenate([w_q, w_in[:, A_WIDTH:o + B_ROPE], pad], axis=1).astype(BF16)

    uq = w_uq.reshape(dq, B_HEADS, B_NOPE + B_ROPE)
    zq = jnp.zeros((dq, B_HEADS, LANES - B_NOPE - B_ROPE), F32)
    wqm = jnp.concatenate([uq, zq], axis=-1).reshape(dq, B_HEADS * LANES).astype(BF16)

    ukv = w_ukv.reshape(dkv, B_HEADS, B_NOPE + B_VDIM)
    wk = jnp.concatenate([ukv[..., :B_NOPE], jnp.zeros((dkv, B_HEADS, LANES - B_NOPE), F32)],
                         axis=-1).reshape(dkv, B_HEADS * LANES).astype(BF16)
    zv = jnp.zeros((dkv, B_HEADS // 2, B_VDIM), F32)
    uv = ukv[..., B_NOPE:].reshape(dkv, B_HEADS // 2, 2, B_VDIM)
    wv = jnp.concatenate([uv[:, :, 0], zv, zv, uv[:, :, 1]], axis=-1).reshape(dkv, B_HEADS * LANES).astype(BF16)

    inv_freq = 1.0 / (ROPE_THETA ** (np.arange(0, B_ROPE, 2, dtype=np.float64) / B_ROPE))
    ang = np.arange(seq, dtype=np.float64)[:, None] * inv_freq[None, :]
    cos, sin = np.cos(ang), np.sin(ang)
    scale = (B_NOPE + B_ROPE) ** -0.5 * LOG2_E
    zr = np.zeros((seq, LANES - B_NOPE - B_ROPE))
    cosq = scale * np.concatenate([np.ones((seq, B_NOPE)), cos, cos, zr], axis=1)
    sinq = scale * np.concatenate([np.zeros((seq, B_NOPE)), -sin, sin, zr], axis=1)
    zk = np.zeros((seq, LANES - B_ROPE))
    cosk = np.concatenate([cos, cos, zk], axis=1)
    sink = np.concatenate([-sin, sin, zk], axis=1)
    f32 = lambda a: a.astype(np.float32)
    return w1, wqm, wk, wv, f32(cosq), f32(sinq), f32(cosk), f32(sink)


def _router_weights(w_r1, b_r1, w_r2, b_r2):
    d = w_r1.shape[0]
    w2 = jnp.transpose(w_r2, (1, 0, 2)).reshape(d, N_EXPERTS)
    padw = jnp.zeros((d, LANES - N_EXPERTS - N_GROUPS), F32)
    wr = jnp.concatenate([w2, w_r1, padw], axis=1)
    wr_hi = wr.astype(BF16)
    wr_lo = (wr - wr_hi.astype(F32)).astype(BF16)
    br = jnp.concatenate([b_r2.reshape(N_EXPERTS), b_r1, jnp.zeros((LANES - N_EXPERTS - N_GROUPS,), F32)])
    return jnp.concatenate([wr_hi, wr_lo], axis=1), wr_hi, br[None, :]


def _layer(h2, p2, batch, seq, g_mix, w_in, g_cq, w_uq, g_ckv, w_ukv, g_out_a, g_out_b, w_o, g_ffn,
           w_r1, b_r1, w_r2, b_r2, w_e_gate, w_e_up, w_e_down, g_ple, w_ple_gate, w_ple_proj, g_out):
    t, d = h2.shape
    assert d == SUBLANES * LANES, "a token's activations must fill exactly one (8, 128) tile"
    w1, wqm, wk, wv, cosq, sinq, cosk, sink = _layer_weights(w_in, w_uq, w_ukv, seq)
    qkv, qf, kf, vb = _in_proj(h2, g_mix[None], w1, g_cq[None], wqm, g_ckv[None], wk, wv,
                               cosq, sinq, cosk, sink, seq)

    slopes = np.exp2(-8.0 * (np.arange(A_HEADS, dtype=np.float64) + 1.0) / A_HEADS)
    for window, dil in DILATED_BRANCHES:
        assert window // (2 * dil) == BAND_HALF
    bias = np.stack([_band_bias(slopes * dil * LOG2_E) for _, dil in DILATED_BRANCHES]).astype(np.float32)
    oa = _dilated(qkv.reshape(batch, seq, -1), bias, batch, seq).reshape(t, A_WIDTH)
    ob = _mla(qf, kf, vb, batch, seq).reshape(t, B_WIDTH)

    wr2, wrh, br = _router_weights(w_r1, b_r1, w_r2, b_r2)
    tri = (jnp.arange(PROJ_TILE)[:, None] > jnp.arange(PROJ_TILE)[None, :]).astype(BF16)
    h1, route, cnt = _post_mix(
        h2, oa, ob, g_out_a[None], g_out_b[None], w_o[:A_WIDTH].astype(BF16), w_o[A_WIDTH:].astype(BF16),
        g_ffn[None], wr2, wrh, br, tri)

    counts = cnt[0].astype(jnp.int32)
    items, offs = _work_items(counts, t)
    xs, pos = _dispatch(offs, route[:, ROUTE_KEY].astype(jnp.int32), route[:, ROUTE_RANK].astype(jnp.int32),
                        h1, route, g_ffn[None])
    ys = _experts(items, xs, w_e_gate.astype(BF16), w_e_up.astype(BF16), w_e_down.astype(BF16))
    return _finish(pos, ys, h1, p2, g_ple[None], w_ple_gate.astype(BF16),
                   w_ple_proj.astype(BF16), g_out[None])


def kernel(x, p, g_mix, w_in, g_cq, w_uq, g_ckv, w_ukv, g_out_a, g_out_b, w_o, g_ffn, w_r1, b_r1, w_r2,
           b_r2, w_e_gate, w_e_up, w_e_down, g_ple, w_ple_gate, w_ple_proj, g_final):
    batch, seq, d = x.shape
    depth = p.shape[0]
    assert depth == 1, "the final norm is fused into the single layer's last kernel"
    h = x.reshape(batch * seq, d)
    i = 0
    h = _layer(h, p[i].reshape(batch * seq, -1), batch, seq, g_mix[i], w_in[i], g_cq[i], w_uq[i],
               g_ckv[i], w_ukv[i], g_out_a[i], g_out_b[i], w_o[i], g_ffn[i], w_r1[i], b_r1[i], w_r2[i],
               b_r2[i], w_e_gate[i], w_e_up[i], w_e_down[i], g_ple[i], w_ple_gate[i], w_ple_proj[i],
               g_final)
    return h.reshape(batch, seq, d)
```

```python
import functools
import math

import jax
import jax.numpy as jnp
import numpy as np
from jax import lax
from jax.experimental import pallas as pl
from jax.experimental.pallas import tpu as pltpu

F32 = jnp.float32
BF16 = jnp.bfloat16

EPS = 1e-6
MASK_VALUE = -1e30
LOG2_E = math.log2(math.e)
LANES = 128
SUBLANES = 8

A_HEADS = 8
A_HEAD_DIM = 64
A_WIDTH = A_HEADS * A_HEAD_DIM
DILATED_BRANCHES = ((128, 1), (512, 4), (2048, 16))

B_HEADS = 8
B_NOPE = 64
B_ROPE = 32
B_VDIM = 64
B_WIDTH = B_HEADS * B_VDIM
ROPE_THETA = 10000.0

N_GROUPS = 4
EXPERTS_PER_GROUP = 8
N_EXPERTS = N_GROUPS * EXPERTS_PER_GROUP
N_PAIR_KEYS = N_EXPERTS * EXPERTS_PER_GROUP
N_PAIRS = N_GROUPS * EXPERTS_PER_GROUP * (EXPERTS_PER_GROUP - 1) // 2

VMEM_LIMIT = 56 * 1024 * 1024

PROJ_TILE = 512
TOKEN_TILE = 256
BAND_TILE = 128
BAND_HALF = 64
BAND_KEYS = 256
BAND_UNROLL = 32
DEINT = 4
MERGE_ROWS = 256
MLA_TQ = 512
MLA_Q_TILES = 2
EXPERT_ROWS = 256
REC_ROWS = 2 * SUBLANES
GATE_ROW = SUBLANES
DMA_ISSUE_UNROLL = 8
ROUTE_GATE_LO, ROUTE_GATE_HI, ROUTE_KEY, ROUTE_RANK = 0, 1, 2, 3


def _rms(x, g):
    return x * lax.rsqrt(jnp.mean(x * x, axis=-1, keepdims=True) + EPS) * g


def _params(*sem):
    return pltpu.CompilerParams(dimension_semantics=sem, vmem_limit_bytes=VMEM_LIMIT)


def _token_cols(ref, n_tokens, rows_per_token, lead=None):
    def rows(c):
        sl = pl.ds(c, n_tokens, stride=rows_per_token)
        return ref[sl, :] if lead is None else ref[lead, sl, :]
    return jnp.concatenate([rows(c) for c in range(SUBLANES)], axis=1)


def _swap_rope_halves(x, first_lane):
    half = B_ROPE // 2
    lane = lax.broadcasted_iota(jnp.int32, x.shape, 1)
    from_above = pltpu.roll(x, LANES - half, 1)
    from_below = pltpu.roll(x, half, 1)
    return jnp.where(lane < first_lane + half, from_above, from_below)


def _in_proj_kernel(x_ref, gmix_ref, w1_ref, gcq_ref, wqm_ref, gckv_ref, wk_ref, wv_ref,
                    cosq_ref, sinq_ref, cosk_ref, sink_ref,
                    qkv_ref, qf_ref, kf_ref, vb_ref, *, dq, dkv):
    a = _rms(x_ref[...], gmix_ref[...]).astype(BF16)
    o = 3 * A_WIDTH
    proj = jnp.dot(a, w1_ref[:, o:], preferred_element_type=F32)
    cq = _rms(proj[:, 0:dq], gcq_ref[...]).astype(BF16)
    ckv = _rms(proj[:, dq:dq + dkv], gckv_ref[...]).astype(BF16)
    kr = proj[:, dq + dkv:dq + dkv + LANES]

    qm = jnp.dot(cq, wqm_ref[...], preferred_element_type=F32)
    cosq = cosq_ref[...]
    sinq = sinq_ref[...]
    for h in range(B_HEADS):
        sl = slice(h * LANES, (h + 1) * LANES)
        qf_ref[:, sl] = (qm[:, sl] * cosq + _swap_rope_halves(qm[:, sl], B_NOPE) * sinq).astype(BF16)

    krot = kr * cosk_ref[...] + _swap_rope_halves(kr, 0) * sink_ref[...]
    krot = pltpu.roll(krot, B_NOPE, 1)
    kf = jnp.dot(ckv, wk_ref[...], preferred_element_type=F32)
    for h in range(B_HEADS):
        sl = slice(h * LANES, (h + 1) * LANES)
        kf_ref[:, sl] = (kf[:, sl] + krot).astype(BF16)
    vb = jnp.dot(ckv, wv_ref[...], preferred_element_type=F32)
    pair_lane = lax.broadcasted_iota(jnp.int32, vb.shape, 1) % (2 * LANES)
    ones_half = (pair_lane >= B_VDIM) & (pair_lane < 2 * LANES - B_VDIM)
    vb_ref[...] = jnp.where(ones_half, 1.0, vb).astype(BF16)
    qkv = jnp.dot(a, w1_ref[:, :o], preferred_element_type=F32)
    qkv_ref[:, :A_WIDTH] = qkv[:, :A_WIDTH] * LOG2_E
    qkv_ref[:, A_WIDTH:] = qkv[:, A_WIDTH:]


def _in_proj(x2, gmix, w1, gcq, wqm, gckv, wk, wv, cosq, sinq, cosk, sink, seq):
    t, d = x2.shape
    tm = PROJ_TILE
    n_pos = seq // tm
    dq, dkv = wqm.shape[0], wk.shape[0]
    full = lambda arr: pl.BlockSpec(arr.shape, lambda i: (0, 0))
    tab = pl.BlockSpec((tm, LANES), lambda i: (i % n_pos, 0))
    row = lambda w: pl.BlockSpec((tm, w), lambda i: (i, 0))
    outs = [(3 * A_WIDTH, F32)] + [(B_HEADS * LANES, BF16)] * 3
    return pl.pallas_call(
        functools.partial(_in_proj_kernel, dq=dq, dkv=dkv),
        grid=(t // tm,),
        in_specs=[row(d), full(gmix), full(w1), full(gcq), full(wqm), full(gckv),
                  full(wk), full(wv), tab, tab, tab, tab],
        out_specs=[row(w) for w, _ in outs],
        out_shape=[jax.ShapeDtypeStruct((t, w), dt) for w, dt in outs],
        compiler_params=_params("parallel"),
        name="in_proj",
    )(x2, gmix, w1, gcq, wqm, gckv, wk, wv, cosq, sinq, cosk, sink)


def _band_bias(slopes_scaled):
    qi = np.arange(BAND_TILE)[:, None]
    kc = np.arange(BAND_KEYS)[None, :]
    variants = []
    for shift in (0, BAND_HALF, BAND_KEYS - BAND_TILE):
        rel = np.abs(kc - shift - qi).astype(np.float64)
        bias = -slopes_scaled[:, None, None] * rel[None]
        variants.append(np.where(rel[None] <= BAND_HALF, bias, MASK_VALUE))
    return np.stack(variants, axis=1)


def _dilated_kernel(q_ref, k_ref, v_ref, bias_ref, out_ref, pv_scr, max_scr, den_scr, deint_scr, *, seq):
    lane = lax.broadcasted_iota(jnp.int32, (BAND_TILE, LANES), 1)
    low_half = lane < A_HEAD_DIM

    sub_len = seq // DEINT
    for a, src in enumerate((q_ref, k_ref, v_ref)):
        for c in range(DEINT):
            deint_scr[a, c * sub_len:(c + 1) * sub_len, :] = src[pl.ds(c, sub_len, stride=DEINT), :]

    for n, (_, dil) in enumerate(DILATED_BRANCHES):
        cls_len = seq // dil
        tiles_per_class = cls_len // BAND_TILE

        def tile(it, carry, n=n, dil=dil, cls_len=cls_len, tiles_per_class=tiles_per_class):
            r = it // tiles_per_class
            l0 = (it % tiles_per_class) * BAND_TILE
            start = jnp.clip(l0 - BAND_HALF, 0, cls_len - BAND_KEYS)
            variant = jnp.where(l0 == 0, 0, jnp.where(l0 == cls_len - BAND_TILE, 2, 1))
            if dil == 1:
                q_rows = pl.ds(pl.multiple_of(l0, BAND_TILE), BAND_TILE)
                k_rows = pl.ds(pl.multiple_of(start, BAND_HALF), BAND_KEYS)
                qp, kp, vp = q_ref[q_rows, :], k_ref[k_rows, :], v_ref[k_rows, :]
            else:
                q_rows = pl.ds(r + dil * l0, BAND_TILE, stride=dil)
                step = dil // DEINT
                base = (r % DEINT) * sub_len + r // DEINT
                if step == 1:
                    q_src = pl.ds(pl.multiple_of(base + l0, BAND_HALF), BAND_TILE)
                    k_src = pl.ds(pl.multiple_of(base + start, BAND_HALF), BAND_KEYS)
                else:
                    q_src = pl.ds(base + step * l0, BAND_TILE, stride=step)
                    k_src = pl.ds(base + step * start, BAND_KEYS, stride=step)
                qp, kp, vp = deint_scr[0, q_src, :], deint_scr[1, k_src, :], deint_scr[2, k_src, :]
            qp, kp, vp = qp.astype(BF16), kp.astype(BF16), vp.astype(BF16)
            pvs, maxes, dens = [], [], []
            for hh in range(2):
                qh = jnp.where(low_half if hh == 0 else ~low_half, qp, jnp.zeros_like(qp))
                s = lax.dot_general(qh, kp, (((1,), (1,)), ((), ())), preferred_element_type=F32)
                s = s + bias_ref[n, hh, variant]
                m = jnp.max(s, axis=-1, keepdims=True)
                p = jnp.exp2(s - m)
                dens.append(jnp.sum(p, axis=-1, keepdims=True))
                maxes.append(m)
                pvs.append(jnp.dot(p.astype(BF16), vp, preferred_element_type=F32))
            pv_scr[n, q_rows, :] = jnp.where(low_half, pvs[0], pvs[1])
            max_scr[n, q_rows, :] = jnp.where(low_half, maxes[0], maxes[1])
            den_scr[n, q_rows, :] = jnp.where(low_half, dens[0], dens[1])
            return carry

        lax.fori_loop(0, seq // BAND_TILE, tile, 0, unroll=BAND_UNROLL)

    def merge(c, carry):
        rows = pl.ds(pl.multiple_of(c * MERGE_ROWS, MERGE_ROWS), MERGE_ROWS)
        maxes = [max_scr[n, rows, :] for n in range(len(DILATED_BRANCHES))]
        top = functools.reduce(jnp.maximum, maxes)
        scales = [jnp.exp2(m - top) for m in maxes]
        num = sum(a * pv_scr[n, rows, :] for n, a in enumerate(scales))
        den = sum(a * den_scr[n, rows, :] for n, a in enumerate(scales))
        out_ref[rows, :] = (num / den).astype(BF16)
        return carry

    lax.fori_loop(0, seq // MERGE_ROWS, merge, 0)


def _dilated(qkv, bias, batch, seq):
    for _, dil in DILATED_BRANCHES:
        assert seq // dil >= BAND_KEYS and (seq // dil) % BAND_TILE == 0
        assert dil == 1 or dil % DEINT == 0
    pairs = A_HEADS // 2
    n_br = len(DILATED_BRANCHES)
    col = lambda base: pl.BlockSpec((None, seq, LANES), lambda b, hp: (b, 0, base + hp))
    return pl.pallas_call(
        functools.partial(_dilated_kernel, seq=seq),
        grid=(batch, pairs),
        in_specs=[col(0), col(pairs), col(2 * pairs),
                  pl.BlockSpec((n_br, 2, 3, BAND_TILE, BAND_KEYS), lambda b, hp: (0, hp, 0, 0, 0))],
        out_specs=col(0),
        out_shape=jax.ShapeDtypeStruct((batch, seq, A_WIDTH), BF16),
        scratch_shapes=[pltpu.VMEM((n_br, seq, LANES), F32)] * 3 + [pltpu.VMEM((3, seq, LANES), F32)],
        compiler_params=_params("parallel", "parallel"),
        name="dilated",
    )(qkv, qkv, qkv, bias)


def _mla_kernel(q_ref, k_ref, v_ref, o_ref):
    lane = lax.broadcasted_iota(jnp.int32, (MLA_TQ, LANES), 1)
    def qk(chain):
        qt, hh = divmod(chain, 2)
        rows = slice(qt * MLA_TQ, (qt + 1) * MLA_TQ)
        sl = slice(hh * LANES, (hh + 1) * LANES)
        return lax.dot_general(q_ref[rows, sl], k_ref[:, sl], (((1,), (1,)), ((), ())),
                               preferred_element_type=F32)

    def softmax_pv(s, hh):
        p = jnp.exp2(s - jnp.max(s, axis=-1, keepdims=True)).astype(BF16)
        pv = jnp.dot(p, v_ref[:, hh * LANES:(hh + 1) * LANES], preferred_element_type=F32)
        return pv / pltpu.roll(pv, B_VDIM, 1)

    n_chains = 2 * MLA_Q_TILES
    outs = []
    s_next = qk(0)
    for chain in range(n_chains):
        s = s_next
        if chain + 1 < n_chains:
            s_next = qk(chain + 1)
        outs.append(softmax_pv(s, chain % 2))
    for qt in range(MLA_Q_TILES):
        rows = slice(qt * MLA_TQ, (qt + 1) * MLA_TQ)
        o_ref[rows, :] = jnp.where(lane < B_VDIM, outs[2 * qt], outs[2 * qt + 1]).astype(BF16)


def _mla(qf, kf, vb, batch, seq):
    pairs = B_HEADS // 2
    rows = MLA_TQ * MLA_Q_TILES
    return pl.pallas_call(
        _mla_kernel,
        grid=(batch, pairs, seq // rows),
        in_specs=[pl.BlockSpec((None, rows, 2 * LANES), lambda b, hp, i: (b, i, hp)),
                  pl.BlockSpec((None, seq, 2 * LANES), lambda b, hp, i: (b, 0, hp)),
                  pl.BlockSpec((None, seq, 2 * LANES), lambda b, hp, i: (b, 0, hp))],
        out_specs=pl.BlockSpec((None, rows, LANES), lambda b, hp, i: (b, i, hp)),
        out_shape=jax.ShapeDtypeStruct((batch, seq, B_WIDTH), BF16),
        compiler_params=_params("parallel", "parallel", "parallel"),
        name="mla",
    )(qf.reshape(batch, seq, -1), kf.reshape(batch, seq, -1), vb.reshape(batch, seq, -1))


def _post_mix_kernel(x_ref, oa_ref, ob_ref, ga_ref, gb_ref, woa_ref, wob_ref, gffn_ref,
                     wr2_ref, wrh_ref, br_ref, tri_ref,
                     h_ref, route_ref, cnt_ref, carry_ref):
    i = pl.program_id(0)

    @pl.when(i == 0)
    def _():
        carry_ref[...] = jnp.zeros_like(carry_ref)

    na = _rms(oa_ref[...].astype(F32), ga_ref[...]).astype(BF16)
    nb = _rms(ob_ref[...].astype(F32), gb_ref[...]).astype(BF16)
    h = x_ref[...] + jnp.dot(na, woa_ref[...], preferred_element_type=F32)
    h = h + jnp.dot(nb, wob_ref[...], preferred_element_type=F32)
    h_ref[...] = h
    m = _rms(h, gffn_ref[...])
    tm = m.shape[0]

    m_hi = m.astype(BF16)
    m_lo = (m - m_hi.astype(F32)).astype(BF16)
    part = jnp.dot(m_hi, wr2_ref[...], preferred_element_type=F32)
    logits = part[:, :LANES] + part[:, LANES:] + br_ref[...]
    logits = logits + jnp.dot(m_lo, wrh_ref[...], preferred_element_type=F32)

    lane_i = lax.broadcasted_iota(jnp.int32, (tm, LANES), 1)
    lane = lane_i.astype(F32)
    big = float(LANES)
    is_group = (lane_i >= N_EXPERTS) & (lane_i < N_EXPERTS + N_GROUPS)
    lg = jnp.where(is_group, logits, -jnp.inf)
    g_max = jnp.max(lg, axis=-1, keepdims=True)
    g_idx = jnp.min(jnp.where(lg == g_max, lane - N_EXPERTS, big), axis=-1, keepdims=True)
    g_val = 1.0 / jnp.sum(jnp.exp(lg - g_max), axis=-1, keepdims=True)

    group_of_lane = (lane_i // EXPERTS_PER_GROUP).astype(F32)
    in_group = (lane_i < N_EXPERTS) & (group_of_lane == g_idx)
    le = jnp.where(in_group, logits, -jnp.inf)
    e_max = jnp.max(le, axis=-1, keepdims=True)
    ee = jnp.exp(le - e_max)
    p2 = jnp.where(in_group, ee / jnp.sum(ee, axis=-1, keepdims=True), -1.0)
    v1 = jnp.max(p2, axis=-1, keepdims=True)
    i1 = jnp.min(jnp.where(p2 == v1, lane, big), axis=-1, keepdims=True)
    p2b = jnp.where(lane == i1, -1.0, p2)
    v2 = jnp.max(p2b, axis=-1, keepdims=True)
    i2 = jnp.min(jnp.where(p2b == v2, lane, big), axis=-1, keepdims=True)
    norm = g_val / (v1 + v2)
    gate1 = v1 * norm
    gate2 = v2 * norm

    first_is_lo = i1 < i2
    e_lo = jnp.where(first_is_lo, i1, i2)
    e_hi = jnp.where(first_is_lo, i2, i1)
    g_lo = jnp.where(first_is_lo, gate1, gate2)
    g_hi = jnp.where(first_is_lo, gate2, gate1)
    key = e_lo * EXPERTS_PER_GROUP + (e_hi - g_idx * EXPERTS_PER_GROUP)

    key_lane = lax.broadcasted_iota(jnp.int32, (tm, N_PAIR_KEYS), 1).astype(F32)
    onehot = key_lane == key
    prefix = jnp.dot(tri_ref[...], onehot.astype(BF16), preferred_element_type=F32)
    carry = carry_ref[...]
    rank = jnp.sum(jnp.where(onehot, carry + prefix, 0.0), axis=-1, keepdims=True)
    carry = carry + jnp.sum(onehot.astype(F32), axis=0, keepdims=True)
    carry_ref[...] = carry
    cnt_ref[...] = jnp.broadcast_to(carry, cnt_ref.shape)

    route = jnp.where(lane_i == ROUTE_GATE_LO, g_lo, jnp.where(lane_i == ROUTE_GATE_HI, g_hi, 0.0))
    route_ref[...] = jnp.where(lane_i == ROUTE_KEY, key, jnp.where(lane_i == ROUTE_RANK, rank, route))


def _post_mix(x2, oa, ob, ga, gb, woa, wob, gffn, wr2, wrh, br, tri):
    t, d = x2.shape
    tm = PROJ_TILE
    full = lambda arr: pl.BlockSpec(arr.shape, lambda i: (0, 0))
    row = lambda w: pl.BlockSpec((tm, w), lambda i: (i, 0))
    return pl.pallas_call(
        _post_mix_kernel,
        grid=(t // tm,),
        in_specs=[row(d), row(A_WIDTH), row(B_WIDTH), full(ga), full(gb), full(woa), full(wob),
                  full(gffn), full(wr2), full(wrh), full(br), full(tri)],
        out_specs=[row(d), row(LANES), pl.BlockSpec((SUBLANES, N_PAIR_KEYS), lambda i: (0, 0))],
        out_shape=[jax.ShapeDtypeStruct((t, d), F32),
                   jax.ShapeDtypeStruct((t, LANES), F32),
                   jax.ShapeDtypeStruct((SUBLANES, N_PAIR_KEYS), F32)],
        scratch_shapes=[pltpu.VMEM((1, N_PAIR_KEYS), F32)],
        compiler_params=_params("arbitrary"),
        name="post_mix",
    )(x2, oa, ob, ga, gb, woa, wob, gffn, wr2, wrh, br, tri)


def _rows_copy(src, src_token, dst, dst_token, rows_per_token, sem):
    src_rows = pl.ds(pl.multiple_of(src_token * rows_per_token, rows_per_token), rows_per_token)
    dst_rows = pl.ds(pl.multiple_of(dst_token * rows_per_token, rows_per_token), rows_per_token)
    return pltpu.make_async_copy(src.at[src_rows, :], dst.at[dst_rows, :], sem)


def _wait_all_rows(vmem_buf, hbm_ref, n_rows, sem, *, to_hbm):
    hbm_rows = hbm_ref.at[pl.ds(0, n_rows), :]
    src, dst = (vmem_buf, hbm_rows) if to_hbm else (hbm_rows, vmem_buf)
    pltpu.make_async_copy(src, dst, sem).wait()


def _with_static_slot(slot, body):
    for s in range(2):
        @pl.when(slot == s)
        def _(s=s):
            body(s)


def _dispatch_kernel(offs_ref, key_ref, rank_ref, h_ref, route_ref, gffn_ref,
                     xs_ref, pos_ref, rec_a, rec_b, rec_c, sems, *, n_tiles):
    i = pl.program_id(0)
    tm = h_ref.shape[0]
    recs = (rec_a, rec_b, rec_c)

    def drain(b):
        _wait_all_rows(recs[b], xs_ref, tm * REC_ROWS, sems.at[b], to_hbm=True)

    def build(b):
        m = _rms(h_ref[...], gffn_ref[...])
        for c in range(SUBLANES):
            recs[b][pl.ds(c, tm, stride=REC_ROWS), :] = m[:, c * LANES:(c + 1) * LANES]
        recs[b][pl.ds(GATE_ROW, tm, stride=REC_ROWS), :] = route_ref[...]

    def scatter(b):
        for j in range(tm):
            pos = offs_ref[key_ref[j]] + rank_ref[j]
            pos_ref[j] = pos
            _rows_copy(recs[b], j, xs_ref, pos, REC_ROWS, sems.at[b]).start()

    for b in range(3):
        prev = (b + 2) % 3

        @pl.when(i % 3 == b)
        def _(b=b, prev=prev):
            @pl.when(i >= 3)
            def _():
                drain(b)

            @pl.when(i < 3)
            def _():
                recs[b][...] = jnp.zeros(recs[b].shape, F32)

            @pl.when((i >= 1) & (i < n_tiles))
            def _():
                build(b)
                scatter(prev)

            if b == 0:
                @pl.when(i == 0)
                def _():
                    build(b)

            if b == n_tiles % 3:
                @pl.when(i == n_tiles)
                def _():
                    scatter(prev)
                    drain(prev)
                    drain((prev + 2) % 3)


def _dispatch(offs, key, rank, h, route, gffn):
    t, d = h.shape
    tm = TOKEN_TILE
    n_tiles = t // tm
    assert n_tiles >= 3
    built = lambda i, of: (jnp.minimum(i, n_tiles - 1), 0)
    scattered = lambda i, of: (jnp.maximum(i - 1, 0),)
    grid_spec = pltpu.PrefetchScalarGridSpec(
        num_scalar_prefetch=1,
        grid=(n_tiles + 1,),
        in_specs=[pl.BlockSpec((tm,), scattered, memory_space=pltpu.SMEM),
                  pl.BlockSpec((tm,), scattered, memory_space=pltpu.SMEM),
                  pl.BlockSpec((tm, d), built),
                  pl.BlockSpec((tm, LANES), built),
                  pl.BlockSpec(gffn.shape, lambda i, of: (0, 0))],
        out_specs=[pl.BlockSpec(memory_space=pl.ANY),
                   pl.BlockSpec((tm,), scattered, memory_space=pltpu.SMEM)],
        scratch_shapes=[pltpu.VMEM((tm * REC_ROWS, LANES), F32)] * 3 + [pltpu.SemaphoreType.DMA((3,))],
    )
    return pl.pallas_call(
        functools.partial(_dispatch_kernel, n_tiles=n_tiles),
        grid_spec=grid_spec,
        out_shape=[jax.ShapeDtypeStruct((t * REC_ROWS, LANES), F32), jax.ShapeDtypeStruct((t,), jnp.int32)],
        compiler_params=_params("arbitrary"),
        name="dispatch",
    )(offs, key, rank, h, route, gffn)


def _ffn(x, wg_ref, wu_ref, wd_ref):
    g = jnp.dot(x, wg_ref[0], preferred_element_type=F32)
    u = jnp.dot(x, wu_ref[0], preferred_element_type=F32)
    hid = (g * (1.0 / (1.0 + jnp.exp(-g))) * u).astype(BF16)
    return jnp.dot(hid, wd_ref[0], preferred_element_type=F32)


ITEM_VALID, ITEM_FIRST, ITEM_LAST = 1, 2, 4


def _experts_kernel(tile_ref, lo_ref, hi_ref, row0_ref, row1_ref, flags_ref,
                    xs_ref, wg_lo, wu_lo, wd_lo, wg_hi, wu_hi, wd_hi, ys_ref, x_scr, gate_scr, y_scr):
    w = pl.program_id(0)
    rows_n = EXPERT_ROWS
    flags = flags_ref[w]

    @pl.when((flags & ITEM_VALID) != 0)
    def _():
        @pl.when((flags & ITEM_FIRST) != 0)
        def _():
            x_scr[...] = _token_cols(xs_ref, rows_n, REC_ROWS).astype(BF16)
            gate_scr[...] = xs_ref[pl.ds(GATE_ROW, rows_n, stride=REC_ROWS), :]

        x = x_scr[...]
        gates = gate_scr[...]
        y = (gates[:, ROUTE_GATE_LO:ROUTE_GATE_LO + 1] * _ffn(x, wg_lo, wu_lo, wd_lo)
             + gates[:, ROUTE_GATE_HI:ROUTE_GATE_HI + 1] * _ffn(x, wg_hi, wu_hi, wd_hi))
        rows = tile_ref[w] * rows_n + lax.broadcasted_iota(jnp.int32, (rows_n, 1), 0)
        mine = (rows >= row0_ref[w]) & (rows < row1_ref[w])

        pltpu.store(y_scr, y, mask=jnp.broadcast_to(mine, y.shape))

        @pl.when((flags & ITEM_LAST) != 0)
        def _():
            for c in range(SUBLANES):
                ys_ref[pl.ds(c, rows_n, stride=SUBLANES), :] = y_scr[:, c * LANES:(c + 1) * LANES]


def _experts(items, xs, wg, wu, wd):
    n_tok = xs.shape[0] // REC_ROWS
    d, ff = wd.shape[2], wd.shape[1]
    up = lambda index_map: pl.BlockSpec((1, d, ff), index_map)
    down = lambda index_map: pl.BlockSpec((1, ff, d), index_map)
    n_items = items[0].shape[0]
    lo = lambda w, ti, lo_e, hi_e, r0, r1, fl: (lo_e[w], 0, 0)
    hi = lambda w, ti, lo_e, hi_e, r0, r1, fl: (hi_e[w], 0, 0)
    tile = lambda w, ti, lo_e, hi_e, r0, r1, fl: (ti[w], 0)
    grid_spec = pltpu.PrefetchScalarGridSpec(
        num_scalar_prefetch=6,
        grid=(n_items,),
        in_specs=[pl.BlockSpec((EXPERT_ROWS * REC_ROWS, LANES), tile),
                  up(lo), up(lo), down(lo), up(hi), up(hi), down(hi)],
        out_specs=pl.BlockSpec((EXPERT_ROWS * SUBLANES, LANES), tile),
        scratch_shapes=[pltpu.VMEM((EXPERT_ROWS, d), BF16), pltpu.VMEM((EXPERT_ROWS, LANES), F32),
                        pltpu.VMEM((EXPERT_ROWS, d), F32)],
    )
    return pl.pallas_call(
        _experts_kernel,
        grid_spec=grid_spec,
        out_shape=jax.ShapeDtypeStruct((n_tok * SUBLANES, LANES), F32),
        compiler_params=_params("arbitrary"),
        name="experts",
    )(*items, xs, wg, wu, wd, wg, wu, wd)


def _work_items(counts, n_rows):
    n_keys = counts.shape[0]
    n_tiles = n_rows // EXPERT_ROWS
    n_items = n_tiles + N_PAIRS - 1
    offs = jnp.concatenate([jnp.zeros((1,), jnp.int32), jnp.cumsum(counts)])
    first_tile = offs[:-1] // EXPERT_ROWS
    last_tile = (offs[1:] - 1) // EXPERT_ROWS
    per_key = jnp.where(counts > 0, last_tile - first_tile + 1, 0)
    item_end = jnp.cumsum(per_key)
    item_start = item_end - per_key
    w = jnp.arange(n_items, dtype=jnp.int32)
    valid = w < item_end[-1]
    key = jnp.sum((item_end[None, :] <= w[:, None]).astype(jnp.int32), axis=1)
    key = jnp.minimum(key, n_keys - 1)
    tile = first_tile[key] + (w - item_start[key])
    last_valid = jnp.maximum(item_end[-1] - 1, 0)
    tile = jnp.where(valid, tile, tile[last_valid]).astype(jnp.int32)
    key = jnp.where(valid, key, key[last_valid])
    changes = tile[1:] != tile[:-1]
    first = jnp.concatenate([jnp.ones((1,), bool), changes])
    last = jnp.concatenate([changes, jnp.ones((1,), bool)]) | (w == last_valid)
    flags = ITEM_VALID * valid + ITEM_FIRST * first + ITEM_LAST * last
    lo = key // EXPERTS_PER_GROUP
    hi = lo // EXPERTS_PER_GROUP * EXPERTS_PER_GROUP + key % EXPERTS_PER_GROUP
    as_i32 = lambda a: a.astype(jnp.int32)
    items = tuple(as_i32(a) for a in (tile, lo, hi, offs[key], offs[key + 1], flags))
    return items, as_i32(offs)


def _finish_kernel(pos_ref, pos_next_ref, ys_ref, h_ref, p_ref, gple_ref, wpg_ref, wpp_ref, gfin_ref,
                   out_ref, rows_ref, sems, *, n_steps):
    i = pl.program_id(0)
    tm = h_ref.shape[0]

    def gather(positions, s):
        def body(j, carry):
            _rows_copy(ys_ref, positions[j], rows_ref.at[s], j, SUBLANES, sems.at[s]).start()
            return carry
        lax.fori_loop(0, tm, body, 0, unroll=DMA_ISSUE_UNROLL)

    def step(s):
        @pl.when(i == 0)
        def _():
            gather(pos_ref, s)

        @pl.when(i + 1 < n_steps)
        def _():
            gather(pos_next_ref, 1 - s)

        _wait_all_rows(rows_ref.at[s], ys_ref, tm * SUBLANES, sems.at[s], to_hbm=False)
        h = h_ref[...] + _token_cols(rows_ref, tm, SUBLANES, lead=s)
        gate_in = _rms(h, gple_ref[...]).astype(BF16)
        z = jnp.dot(gate_in, wpg_ref[...], preferred_element_type=F32)
        gate = 1.0 / (1.0 + jnp.exp(-z))
        h = h + gate * jnp.dot(p_ref[...].astype(BF16), wpp_ref[...], preferred_element_type=F32)
        out_ref[...] = _rms(h, gfin_ref[...])

    _with_static_slot(i % 2, step)


def _finish(pos, ys, h, p2, gple, wpg, wpp, gfin):
    t, d = h.shape
    tm = TOKEN_TILE
    full = lambda arr: pl.BlockSpec(arr.shape, lambda i: (0, 0))
    row = lambda w: pl.BlockSpec((tm, w), lambda i: (i, 0))
    n_steps = t // tm
    return pl.pallas_call(
        functools.partial(_finish_kernel, n_steps=n_steps),
        grid=(n_steps,),
        in_specs=[pl.BlockSpec((tm,), lambda i: (i,), memory_space=pltpu.SMEM),
                  pl.BlockSpec((tm,), lambda i: (jnp.minimum(i + 1, n_steps - 1),), memory_space=pltpu.SMEM),
                  pl.BlockSpec(memory_space=pl.ANY), row(d), row(p2.shape[1]),
                  full(gple), full(wpg), full(wpp), full(gfin)],
        out_specs=row(d),
        out_shape=jax.ShapeDtypeStruct((t, d), F32),
        scratch_shapes=[pltpu.VMEM((2, tm * SUBLANES, LANES), F32), pltpu.SemaphoreType.DMA((2,))],
        compiler_params=_params("arbitrary"),
        name="finish",
    )(pos, pos, ys, h, p2, gple, wpg, wpp, gfin)


def _layer_weights(w_in, w_uq, w_ukv, seq):
    d = w_in.shape[0]
    dq, dkv = w_uq.shape[0], w_ukv.shape[0]
    o = 3 * A_WIDTH + dq + dkv
    pad = jnp.zeros((d, LANES - B_ROPE), F32)
    w_q = w_in[:, :A_WIDTH] * (A_HEAD_DIM ** -0.5)
    w1 = jnp.concatenate([w_q, w_in[:, A_WIDTH:o + B_ROPE], pad], axis=1).astype(BF16)

    uq = w_uq.reshape(dq, B_HEADS, B_NOPE + B_ROPE)
    zq = jnp.zeros((dq, B_HEADS, LANES - B_NOPE - B_ROPE), F32)
    wqm = jnp.concatenate([uq, zq], axis=-1).reshape(dq, B_HEADS * LANES).astype(BF16)

    ukv = w_ukv.reshape(dkv, B_HEADS, B_NOPE + B_VDIM)
    wk = jnp.concatenate([ukv[..., :B_NOPE], jnp.zeros((dkv, B_HEADS, LANES - B_NOPE), F32)],
                         axis=-1).reshape(dkv, B_HEADS * LANES).astype(BF16)
    zv = jnp.zeros((dkv, B_HEADS // 2, B_VDIM), F32)
    uv = ukv[..., B_NOPE:].reshape(dkv, B_HEADS // 2, 2, B_VDIM)
    wv = jnp.concatenate([uv[:, :, 0], zv, zv, uv[:, :, 1]], axis=-1).reshape(dkv, B_HEADS * LANES).astype(BF16)

    inv_freq = 1.0 / (ROPE_THETA ** (np.arange(0, B_ROPE, 2, dtype=np.float64) / B_ROPE))
    ang = np.arange(seq, dtype=np.float64)[:, None] * inv_freq[None, :]
    cos, sin = np.cos(ang), np.sin(ang)
    scale = (B_NOPE + B_ROPE) ** -0.5 * LOG2_E
    zr = np.zeros((seq, LANES - B_NOPE - B_ROPE))
    cosq = scale * np.concatenate([np.ones((seq, B_NOPE)), cos, cos, zr], axis=1)
    sinq = scale * np.concatenate([np.zeros((seq, B_NOPE)), -sin, sin, zr], axis=1)
    zk = np.zeros((seq, LANES - B_ROPE))
    cosk = np.concatenate([cos, cos, zk], axis=1)
    sink = np.concatenate([-sin, sin, zk], axis=1)
    f32 = lambda a: a.astype(np.float32)
    return w1, wqm, wk, wv, f32(cosq), f32(sinq), f32(cosk), f32(sink)


def _router_weights(w_r1, b_r1, w_r2, b_r2):
    d = w_r1.shape[0]
    w2 = jnp.transpose(w_r2, (1, 0, 2)).reshape(d, N_EXPERTS)
    padw = jnp.zeros((d, LANES - N_EXPERTS - N_GROUPS), F32)
    wr = jnp.concatenate([w2, w_r1, padw], axis=1)
    wr_hi = wr.astype(BF16)
    wr_lo = (wr - wr_hi.astype(F32)).astype(BF16)
    br = jnp.concatenate([b_r2.reshape(N_EXPERTS), b_r1, jnp.zeros((LANES - N_EXPERTS - N_GROUPS,), F32)])
    return jnp.concatenate([wr_hi, wr_lo], axis=1), wr_hi, br[None, :]


def _layer(h2, p2, batch, seq, g_mix, w_in, g_cq, w_uq, g_ckv, w_ukv, g_out_a, g_out_b, w_o, g_ffn,
           w_r1, b_r1, w_r2, b_r2, w_e_gate, w_e_up, w_e_down, g_ple, w_ple_gate, w_ple_proj, g_out):
    t, d = h2.shape
    assert d == SUBLANES * LANES, "a token's activations must fill exactly one (8, 128) tile"
    w1, wqm, wk, wv, cosq, sinq, cosk, sink = _layer_weights(w_in, w_uq, w_ukv, seq)
    qkv, qf, kf, vb = _in_proj(h2, g_mix[None], w1, g_cq[None], wqm, g_ckv[None], wk, wv,
                               cosq, sinq, cosk, sink, seq)

    slopes = np.exp2(-8.0 * (np.arange(A_HEADS, dtype=np.float64) + 1.0) / A_HEADS)
    for window, dil in DILATED_BRANCHES:
        assert window // (2 * dil) == BAND_HALF
    bias = np.stack([_band_bias(slopes * dil * LOG2_E) for _, dil in DILATED_BRANCHES]).astype(np.float32)
    oa = _dilated(qkv.reshape(batch, seq, -1), bias, batch, seq).reshape(t, A_WIDTH)
    ob = _mla(qf, kf, vb, batch, seq).reshape(t, B_WIDTH)

    wr2, wrh, br = _router_weights(w_r1, b_r1, w_r2, b_r2)
    tri = (jnp.arange(PROJ_TILE)[:, None] > jnp.arange(PROJ_TILE)[None, :]).astype(BF16)
    h1, route, cnt = _post_mix(
        h2, oa, ob, g_out_a[None], g_out_b[None], w_o[:A_WIDTH].astype(BF16), w_o[A_WIDTH:].astype(BF16),
        g_ffn[None], wr2, wrh, br, tri)

    counts = cnt[0].astype(jnp.int32)
    items, offs = _work_items(counts, t)
    xs, pos = _dispatch(offs, route[:, ROUTE_KEY].astype(jnp.int32), route[:, ROUTE_RANK].astype(jnp.int32),
                        h1, route, g_ffn[None])
    ys = _experts(items, xs, w_e_gate.astype(BF16), w_e_up.astype(BF16), w_e_down.astype(BF16))
    return _finish(pos, ys, h1, p2, g_ple[None], w_ple_gate.astype(BF16),
                   w_ple_proj.astype(BF16), g_out[None])


def kernel(x, p, g_mix, w_in, g_cq, w_uq, g_ckv, w_ukv, g_out_a, g_out_b, w_o, g_ffn, w_r1, b_r1, w_r2,
           b_r2, w_e_gate, w_e_up, w_e_down, g_ple, w_ple_gate, w_ple_proj, g_final):
    batch, seq, d = x.shape
    depth = p.shape[0]
    assert depth == 1, "the final norm is fused into the single layer's last kernel"
    h = x.reshape(batch * seq, d)
    i = 0
    h = _layer(h, p[i].reshape(batch * seq, -1), batch, seq, g_mix[i], w_in[i], g_cq[i], w_uq[i],
               g_ckv[i], w_ukv[i], g_out_a[i], g_out_b[i], w_o[i], g_ffn[i], w_r1[i], b_r1[i], w_r2[i],
               b_r2[i], w_e_gate[i], w_e_up[i], w_e_down[i], g_ple[i], w_ple_gate[i], w_ple_proj[i],
               g_final)
    return h.reshape(batch, seq, d)
```
